```python
import jax, jax.numpy as jnp
from jax import lax
import numpy as np

D_MODEL = 1024
BATCH = 4
SEQ = 4096
DEPTH = 1

D_MIX = D_MODEL
SWA_WIDTH = D_MIX // 2
SWA_HEAD_DIM = 64
SWA_Q_HEADS = SWA_WIDTH // SWA_HEAD_DIM
SWA_KV_HEADS = 2
SWA_GROUP = SWA_Q_HEADS // SWA_KV_HEADS
WINDOW = 128
BLOCK = 128
ROPE_THETA = 500000.0
ROT_DIM = SWA_HEAD_DIM // 4
GLA_WIDTH = D_MIX - SWA_WIDTH
GLA_HEADS = 4
GLA_DK = GLA_WIDTH // 2 // GLA_HEADS
GLA_DV = GLA_WIDTH // GLA_HEADS
GLA_RANK = 16
GLA_TAU = 16.0
GLA_CHUNK = 64
IN_SPLITS = (
    SWA_Q_HEADS * SWA_HEAD_DIM,
    SWA_KV_HEADS * SWA_HEAD_DIM,
    SWA_KV_HEADS * SWA_HEAD_DIM,
    SWA_WIDTH,
    GLA_HEADS * GLA_DK,
    GLA_HEADS * GLA_DK,
    GLA_HEADS * GLA_DV,
    GLA_WIDTH,
    GLA_RANK,
)
D_IN_PROJ = sum(IN_SPLITS)
EPS = 1e-5
ALPHA = (2 * DEPTH) ** 0.25
BETA = (8 * DEPTH) ** -0.25

kernel_name = 'hymba_swa_sink_gla_deepnorm'


def split_cols(t, sizes):
    out, start = [], 0
    for s in sizes:
        out.append(t[..., start:start + s])
        start += s
    return out


def partial_rope(t, pos):
    half = ROT_DIM // 2
    inv_freq = ROPE_THETA ** (-jnp.arange(half, dtype=jnp.float32) / half)
    ang = pos.astype(jnp.float32)[..., None] * inv_freq
    cos = jnp.cos(ang)[:, :, None, :].astype(t.dtype)
    sin = jnp.sin(ang)[:, :, None, :].astype(t.dtype)
    t1 = t[..., :half]
    t2 = t[..., half:ROT_DIM]
    return jnp.concatenate([t1 * cos - t2 * sin, t2 * cos + t1 * sin, t[..., ROT_DIM:]], axis=-1)


def sliding_window_attention(q, k, v, sinks):
    B, S = q.shape[0], q.shape[1]
    nb = S // BLOCK
    qb = q.reshape(B, nb, BLOCK, SWA_KV_HEADS, SWA_GROUP, SWA_HEAD_DIM)

    def with_prev(t):
        tb = t.reshape(B, nb, BLOCK, SWA_KV_HEADS, SWA_HEAD_DIM)
        prev = jnp.concatenate([jnp.zeros_like(tb[:, :1]), tb[:, :-1]], axis=1)
        return jnp.concatenate([prev, tb], axis=2)

    kb = with_prev(k)
    vb = with_prev(v)
    scale = SWA_HEAD_DIM ** -0.5
    scores = jnp.einsum('bnqhgd,bnkhd->bnhgqk', qb, kb).astype(jnp.float32) * scale
    qi = jnp.arange(BLOCK)[:, None]
    ki = jnp.arange(2 * BLOCK)[None, :]
    dist = qi + BLOCK - ki
    in_window = (dist >= 0) & (dist < WINDOW)
    has_prev = (jnp.arange(nb)[:, None, None] > 0) | (ki >= BLOCK)[None]
    mask = in_window[None] & has_prev
    scores = jnp.where(mask[None, :, None, None], scores, -jnp.inf)
    sink = sinks.astype(jnp.float32).reshape(SWA_KV_HEADS, SWA_GROUP)[None, None, :, :, None, None]
    m = jnp.maximum(scores.max(axis=-1, keepdims=True), sink)
    p = jnp.exp(scores - m)
    denom = p.sum(axis=-1, keepdims=True) + jnp.exp(sink - m)
    probs = (p / denom).astype(v.dtype)
    out = jnp.einsum('bnhgqk,bnkhd->bnqhgd', probs, vb)
    return out.reshape(B, S, SWA_Q_HEADS * SWA_HEAD_DIM)


def gla_chunked(q, k, v, log_a):
    B, S = q.shape[0], q.shape[1]
    C = GLA_CHUNK
    nc = S // C

    def chunks(t):
        return t.reshape(B, nc, C, t.shape[2], t.shape[3]).astype(jnp.float32)

    qc = chunks(q) * (GLA_DK ** -0.5)
    kc = chunks(k)
    vc = chunks(v)
    b = jnp.cumsum(chunks(log_a), axis=2)
    b_last = b[:, :, -1:]
    q_dec = qc * jnp.exp(b)
    k_inv = kc * jnp.exp(-b)
    k_to_end = kc * jnp.exp(b_last - b)
    causal = jnp.tril(jnp.ones((C, C), dtype=bool))
    attn = jnp.einsum('bnihd,bnjhd->bnhij', q_dec, k_inv)
    attn = jnp.where(causal, attn, 0.0)
    o_intra = jnp.einsum('bnhij,bnjhv->bnihv', attn, vc)
    upd = jnp.einsum('bnjhd,bnjhv->bnhdv', k_to_end, vc)
    decay = jnp.exp(b_last[:, :, 0])

    def step(state, inp):
        dec, u = inp
        return state * dec[..., None] + u, state

    init = jnp.zeros((B, GLA_HEADS, GLA_DK, GLA_DV), jnp.float32)
    _, s_prev = lax.scan(step, init, (jnp.swapaxes(decay, 0, 1), jnp.swapaxes(upd, 0, 1)))
    s_prev = jnp.swapaxes(s_prev, 0, 1)
    o_inter = jnp.einsum('bnihd,bnhdv->bnihv', q_dec, s_prev)
    return (o_intra + o_inter).reshape(B, S, GLA_HEADS, GLA_DV)


def layer_norm(t, g, b):
    tf = t.astype(jnp.float32)
    mu = tf.mean(axis=-1, keepdims=True)
    var = jnp.square(tf - mu).mean(axis=-1, keepdims=True)
    return ((tf - mu) * lax.rsqrt(var + EPS) * g.astype(jnp.float32) + b.astype(jnp.float32)).astype(t.dtype)


def setup_inputs(seed: int = 0) -> dict:
    key = jax.random.key(seed)
    ks = jax.random.split(key, 12)
    x = jax.random.normal(ks[0], (BATCH, SEQ, D_MODEL), jnp.float32)
    offset = jax.random.randint(ks[1], (BATCH, 1), 0, 1024, dtype=jnp.int32)
    positions = offset + jnp.arange(SEQ, dtype=jnp.int32)[None, :]
    w_in = jax.random.normal(ks[2], (DEPTH, D_MODEL, D_IN_PROJ), jnp.float32) * D_MODEL ** -0.5
    starts = np.cumsum((0,) + IN_SPLITS)
    col_scale = np.ones((D_IN_PROJ,), np.float32)
    col_scale[starts[2]:starts[3]] = BETA
    col_scale[starts[6]:starts[7]] = BETA
    w_in = w_in * jnp.asarray(col_scale)
    gla_w_gate_up = jax.random.normal(ks[3], (DEPTH, GLA_RANK, GLA_HEADS * GLA_DK), jnp.float32) * GLA_RANK ** -0.5
    gla_b_gate = 0.01 * jax.random.normal(ks[4], (DEPTH, GLA_HEADS * GLA_DK), jnp.float32)
    attn_sinks = 0.5 * jax.random.normal(ks[5], (DEPTH, SWA_Q_HEADS), jnp.float32)
    gla_norm_w = 1.0 + 0.01 * jax.random.normal(ks[6], (DEPTH, GLA_DV), jnp.float32)
    w_out = jax.random.normal(ks[7], (DEPTH, D_MIX, D_MODEL), jnp.float32) * (D_MIX ** -0.5) * BETA
    ln_g = 1.0 + 0.01 * jax.random.normal(ks[8], (DEPTH, D_MODEL), jnp.float32)
    ln_b = 0.01 * jax.random.normal(ks[9], (DEPTH, D_MODEL), jnp.float32)
    return {'x': x, 'positions': positions, 'w_in': w_in, 'gla_w_gate_up': gla_w_gate_up,
            'gla_b_gate': gla_b_gate, 'attn_sinks': attn_sinks, 'gla_norm_w': gla_norm_w,
            'w_out': w_out, 'ln_g': ln_g, 'ln_b': ln_b}


def reference(x, positions, w_in, gla_w_gate_up, gla_b_gate, attn_sinks, gla_norm_w, w_out, ln_g, ln_b):
    B, S = x.shape[0], x.shape[1]
    for layer in range(DEPTH):
        proj = jnp.einsum('bsd,de->bse', x, w_in[layer])
        q_a, k_a, v_a, g_a, q_b, k_b, v_b, g_b, r_b = split_cols(proj, IN_SPLITS)
        q_a = partial_rope(q_a.reshape(B, S, SWA_Q_HEADS, SWA_HEAD_DIM), positions)
        k_a = partial_rope(k_a.reshape(B, S, SWA_KV_HEADS, SWA_HEAD_DIM), positions)
        v_a = v_a.reshape(B, S, SWA_KV_HEADS, SWA_HEAD_DIM)
        out_a = sliding_window_attention(q_a, k_a, v_a, attn_sinks[layer]) * jax.nn.silu(g_a)
        gate_logit = jnp.einsum('bsr,re->bse', r_b, gla_w_gate_up[layer]) + gla_b_gate[layer]
        log_a = jax.nn.log_sigmoid(gate_logit.astype(jnp.float32)) / GLA_TAU
        o_b = gla_chunked(q_b.reshape(B, S, GLA_HEADS, GLA_DK),
                          k_b.reshape(B, S, GLA_HEADS, GLA_DK),
                          v_b.reshape(B, S, GLA_HEADS, GLA_DV),
                          log_a.reshape(B, S, GLA_HEADS, GLA_DK))
        o_b = o_b * lax.rsqrt(jnp.mean(jnp.square(o_b), axis=-1, keepdims=True) + EPS) * gla_norm_w[layer].astype(jnp.float32)
        out_b = o_b.reshape(B, S, GLA_WIDTH).astype(x.dtype) * jax.nn.silu(g_b)
        mix = jnp.einsum('bse,ed->bsd', jnp.concatenate([out_a, out_b], axis=-1), w_out[layer])
        x = layer_norm(ALPHA * x + mix, ln_g[layer], ln_b[layer])
    return x
```

```python
import functools

import numpy as np
import jax
import jax.numpy as jnp
from jax import lax
from jax.experimental import pallas as pl
from jax.experimental.pallas import tpu as pltpu

D_MODEL = 1024
SWA_WIDTH = 512
SWA_HEAD_DIM = 64
SWA_Q_HEADS = 8
SWA_KV_HEADS = 2
WINDOW = 128
BLOCK = 128
ROPE_THETA = 500000.0
ROT_DIM = 16
GLA_WIDTH = 512
GLA_HEADS = 4
GLA_DK = 64
GLA_DV = 128
GLA_RANK = 16
GLA_TAU = 16.0
GLA_CHUNK = 64
EPS = 1e-5
DEPTH = 1
ALPHA = (2 * DEPTH) ** 0.25

LANES = 128
HALF = LANES // 2

QA, KA, VA, GA, QB, KB, VB, GB, RB = 0, 512, 640, 768, 1280, 1536, 1792, 2304, 2816
D_IN_PROJ = RB + GLA_RANK
D_IN_PAD = RB + LANES

TILE = 256
VMEM_LIMIT_BYTES = 48 * 1024 * 1024

F32 = jnp.float32
BF16 = jnp.bfloat16


def _nt(a, b):
    return lax.dot_general(a, b, (((1,), (1,)), ((), ())), preferred_element_type=F32)


def _tn(a, b):
    return lax.dot_general(a, b, (((0,), (0,)), ((), ())), preferred_element_type=F32)


def _mm(a, b):
    return jnp.dot(a, b, preferred_element_type=F32)


def _layer_kernel(x_ref, pos_ref, w_in_ref, w_up_ref, bg_ref, nw_ref, w_out_ref,
                  lng_ref, lnb_ref, invf_ref, sinks_ref,
                  o_ref, kprev_ref, vprev_ref, s_ref, mix_ref):
    t_idx = pl.program_id(1)

    @pl.when(t_idx == 0)
    def _reset_carries():
        kprev_ref[...] = jnp.zeros_like(kprev_ref)
        vprev_ref[...] = jnp.zeros_like(vprev_ref)
        s_ref[...] = jnp.zeros_like(s_ref)

    x = x_ref[...]
    xb = x.astype(BF16)

    def proj(lo, width=LANES):
        return _mm(xb, w_in_ref[:, lo:lo + width])

    lane = lax.broadcasted_iota(jnp.int32, (1, LANES), 1)
    lo_half = lane < HALF
    lane_in_head = lane & (HALF - 1)

    pos = pos_ref[...].astype(F32)
    ang = pos * invf_ref[...]
    cos = jnp.cos(ang)
    sin = jnp.sin(ang)
    sin_up = jnp.where(lane_in_head >= ROT_DIM // 2, sin, 0.0)
    sin_dn = jnp.where(lane_in_head < ROT_DIM // 2, -sin, 0.0)

    def rope(t):
        return (t * cos + pltpu.roll(t, ROT_DIM // 2, 1) * sin_up
                + pltpu.roll(t, LANES - ROT_DIM // 2, 1) * sin_dn)

    n_blk = TILE // BLOCK
    k_new = rope(proj(KA))
    v_new = proj(VA)
    k_cat = jnp.concatenate([kprev_ref[...], k_new], axis=0)
    v_cat = jnp.concatenate([vprev_ref[...], v_new], axis=0)
    kprev_ref[...] = k_new[TILE - BLOCK:]
    vprev_ref[...] = v_new[TILE - BLOCK:]

    def head_copies(t):
        swapped = pltpu.roll(t, HALF, 1)
        tb, sb = t.astype(BF16), swapped.astype(BF16)
        zero = jnp.zeros_like(tb)
        lo = [jnp.where(lo_half, tb, zero), jnp.where(lo_half, sb, zero)]
        hi = [jnp.where(lo_half, zero, sb), jnp.where(lo_half, zero, tb)]
        return lo, hi

    k_lo, k_hi = head_copies(k_cat)
    v_lo, v_hi = head_copies(v_cat)

    qi = lax.broadcasted_iota(jnp.int32, (BLOCK, 2 * BLOCK), 0)
    ki = lax.broadcasted_iota(jnp.int32, (BLOCK, 2 * BLOCK), 1)
    in_window = (ki > qi) & (ki <= qi + WINDOW)
    first_blk_mask = in_window & ((ki >= BLOCK) | (t_idx > 0))

    scale = SWA_HEAD_DIM ** -0.5
    for cg in range(SWA_WIDTH // LANES):
        h = cg // 2
        q2 = (rope(proj(QA + cg * LANES)) * scale).astype(BF16)
        gate = jax.nn.silu(proj(GA + cg * LANES))
        for i in range(n_blk):
            rows = slice(i * BLOCK, (i + 1) * BLOCK)
            kv_rows = slice(i * BLOCK, (i + 2) * BLOCK)
            mask = first_blk_mask if i == 0 else in_window
            acc = None
            invs = []
            for par, (kk, vv) in enumerate(((k_lo[h], v_lo[h]), (k_hi[h], v_hi[h]))):
                sink = sinks_ref[2 * cg + par]
                s = _nt(q2[rows], kk[kv_rows])
                s = jnp.where(mask, s, -jnp.inf)
                m = jnp.maximum(jnp.max(s, axis=-1, keepdims=True), sink)
                p = jnp.exp(s - m)
                denom = jnp.sum(p, axis=-1, keepdims=True) + jnp.exp(sink - m)
                invs.append(1.0 / denom)
                pv = _mm(p.astype(BF16), vv[kv_rows])
                acc = pv if acc is None else acc + pv
            out = acc * jnp.where(lo_half, invs[0], invs[1]) * gate[rows]
            mix_ref[rows, cg * LANES:(cg + 1) * LANES] = out.astype(BF16)

    n_chunk = TILE // GLA_CHUNK
    r_b = proj(RB).astype(BF16)
    logit = _mm(r_b, w_up_ref[...]) + bg_ref[...]
    log_a = jax.nn.log_sigmoid(logit) / GLA_TAU

    ri = lax.broadcasted_iota(jnp.int32, (TILE, TILE), 0)
    ci = lax.broadcasted_iota(jnp.int32, (TILE, TILE), 1)
    tri = ((ri // GLA_CHUNK == ci // GLA_CHUNK) & (ci <= ri)).astype(BF16)
    la_hi = log_a.astype(BF16)
    la_lo = (log_a - la_hi.astype(F32)).astype(BF16)
    b = _mm(tri, la_hi) + _mm(tri, la_lo)

    q_b = jnp.concatenate([proj(QB), proj(QB + LANES)], axis=1)
    k_b = jnp.concatenate([proj(KB), proj(KB + LANES)], axis=1)
    q_dec = (q_b * (GLA_DK ** -0.5) * jnp.exp(b)).astype(BF16)
    k_inv = (k_b * jnp.exp(-b)).astype(BF16)

    ti = lax.broadcasted_iota(jnp.int32, (GLA_CHUNK, GLA_CHUNK), 0)
    tj = lax.broadcasted_iota(jnp.int32, (GLA_CHUNK, GLA_CHUNK), 1)
    causal = tj <= ti
    eye = (lax.broadcasted_iota(jnp.int32, (LANES, LANES), 0)
           == lax.broadcasted_iota(jnp.int32, (LANES, LANES), 1))
    norm_w = nw_ref[...]

    for pair in range(GLA_HEADS // 2):
        cols = slice(pair * LANES, (pair + 1) * LANES)
        state = s_ref[cols, :]
        v2 = [proj(VB + (2 * pair + par) * GLA_DV).astype(BF16) for par in range(2)]
        g2 = [jax.nn.silu(proj(GB + (2 * pair + par) * GLA_DV)) for par in range(2)]
        for c in range(n_chunk):
            rows = slice(c * GLA_CHUNK, (c + 1) * GLA_CHUNK)
            last = (c + 1) * GLA_CHUNK - 1
            b_c = b[rows, cols]
            b_last = b[last:last + 1, cols]
            q2 = q_dec[rows, cols]
            ki2 = k_inv[rows, cols]
            k_end = (k_b[rows, cols] * jnp.exp(b_last - b_c)).astype(BF16)
            state_b = state.astype(BF16)
            zero = jnp.zeros_like(q2)
            upd = []
            for par in range(2):
                head = 2 * pair + par
                sel = lo_half if par == 0 else jnp.logical_not(lo_half)
                attn = _nt(q2, jnp.where(sel, ki2, zero))
                attn = jnp.where(causal, attn, 0.0).astype(BF16)
                o = _mm(attn, v2[par][rows]) + _mm(jnp.where(sel, q2, zero), state_b)
                o = o * lax.rsqrt(jnp.mean(jnp.square(o), axis=-1, keepdims=True) + EPS) * norm_w
                out = o * g2[par][rows]
                c0 = SWA_WIDTH + head * GLA_DV
                mix_ref[rows, c0:c0 + GLA_DV] = out.astype(BF16)
                upd.append(_tn(k_end, v2[par][rows]))
            decay_row = jnp.broadcast_to(jnp.exp(b_last), (LANES, LANES))
            decay_col = jnp.sum(jnp.where(eye, decay_row, 0.0), axis=1, keepdims=True)
            row_lo = lax.broadcasted_iota(jnp.int32, (LANES, 1), 0) < GLA_DK
            state = state * decay_col + jnp.where(row_lo, upd[0], upd[1])
        s_ref[cols, :] = state

    y = _mm(mix_ref[...], w_out_ref[...]) + ALPHA * x
    mu = jnp.mean(y, axis=-1, keepdims=True)
    yc = y - mu
    var = jnp.mean(jnp.square(yc), axis=-1, keepdims=True)
    o_ref[...] = yc * lax.rsqrt(var + EPS) * lng_ref[...] + lnb_ref[...]


def _rope_lane_freqs():
    half = ROT_DIM // 2
    inv_freq = ROPE_THETA ** (-jnp.arange(half, dtype=F32) / half)
    lane = np.arange(LANES)
    in_head = lane % HALF
    table = jnp.where(in_head < ROT_DIM, inv_freq[in_head % half], 0.0)
    return table.reshape(1, LANES).astype(F32)


@jax.jit
def kernel(x, positions, w_in, gla_w_gate_up, gla_b_gate, attn_sinks, gla_norm_w, w_out, ln_g, ln_b):
    B, S, D = x.shape
    assert D == D_MODEL and S % TILE == 0 and w_in.shape == (DEPTH, D_MODEL, D_IN_PROJ)
    w_in_p = jnp.pad(w_in[0].astype(BF16), ((0, 0), (0, D_IN_PAD - D_IN_PROJ)))
    w_up_p = jnp.pad(gla_w_gate_up[0].astype(BF16), ((0, LANES - GLA_RANK), (0, 0)))
    pos3 = positions.reshape(B, S, 1)

    const = lambda shape: pl.BlockSpec(shape, lambda b, t: (0,) * len(shape))
    grid_spec = pl.GridSpec(
        grid=(B, S // TILE),
        in_specs=[
            pl.BlockSpec((None, TILE, D_MODEL), lambda b, t: (b, t, 0)),
            pl.BlockSpec((None, TILE, 1), lambda b, t: (b, t, 0)),
            const((D_MODEL, D_IN_PAD)),
            const((LANES, GLA_HEADS * GLA_DK)),
            const((1, GLA_HEADS * GLA_DK)),
            const((1, GLA_DV)),
            const((D_MODEL, D_MODEL)),
            const((1, D_MODEL)),
            const((1, D_MODEL)),
            const((1, LANES)),
            pl.BlockSpec(memory_space=pltpu.SMEM),
        ],
        out_specs=pl.BlockSpec((None, TILE, D_MODEL), lambda b, t: (b, t, 0)),
        scratch_shapes=[
            pltpu.VMEM((BLOCK, LANES), F32),
            pltpu.VMEM((BLOCK, LANES), F32),
            pltpu.VMEM((GLA_HEADS * GLA_DK, GLA_DV), F32),
            pltpu.VMEM((TILE, D_MODEL), BF16),
        ],
    )
    return pl.pallas_call(
        _layer_kernel,
        grid_spec=grid_spec,
        out_shape=jax.ShapeDtypeStruct((B, S, D_MODEL), x.dtype),
        compiler_params=pltpu.CompilerParams(
            dimension_semantics=("arbitrary", "arbitrary"),
            vmem_limit_bytes=VMEM_LIMIT_BYTES),
        name="hymba_layer",
    )(x, pos3, w_in_p, w_up_p, gla_b_gate[0].reshape(1, -1), gla_norm_w[0].reshape(1, -1),
      w_out[0].astype(BF16), ln_g[0].reshape(1, -1), ln_b[0].reshape(1, -1),
      _rope_lane_freqs(), attn_sinks[0])
```

```python
import numpy as np
import jax
import jax.numpy as jnp
from jax import lax
from jax.experimental import pallas as pl
from jax.experimental.pallas import tpu as pltpu

D_MODEL = 1024
SWA_WIDTH = 512
SWA_HEAD_DIM = 64
SWA_Q_HEADS = 8
SWA_KV_HEADS = 2
WINDOW = 128
BLOCK = 128
ROPE_THETA = 500000.0
ROT_DIM = 16
GLA_WIDTH = 512
GLA_HEADS = 4
GLA_DK = 64
GLA_DV = 128
GLA_RANK = 16
GLA_TAU = 16.0
GLA_CHUNK = 64
EPS = 1e-5
DEPTH = 1
ALPHA = (2 * DEPTH) ** 0.25

LANES = 128
HALF = LANES // 2

QA, KA, VA, GA, QB, KB, VB, GB, RB = 0, 512, 640, 768, 1280, 1536, 1792, 2304, 2816
D_IN_PROJ = RB + GLA_RANK
D_IN_PAD = RB + LANES

TILE = 256
VMEM_LIMIT_BYTES = 48 * 1024 * 1024

F32 = jnp.float32
BF16 = jnp.bfloat16


def _nt(a, b):
    return lax.dot_general(a, b, (((1,), (1,)), ((), ())), preferred_element_type=F32)


def _tn(a, b):
    return lax.dot_general(a, b, (((0,), (0,)), ((), ())), preferred_element_type=F32)


def _mm(a, b):
    return jnp.dot(a, b, preferred_element_type=F32)


def _layer_kernel(x_ref, pos_ref, w_in_ref, w_up_ref, bg_ref, nw_ref, w_out_ref,
                  lng_ref, lnb_ref, invf_ref, sinks_ref,
                  o_ref, kprev_ref, vprev_ref, s_ref, mix_ref):
    t_idx = pl.program_id(1)

    @pl.when(t_idx == 0)
    def _reset_carries():
        kprev_ref[...] = jnp.zeros_like(kprev_ref)
        vprev_ref[...] = jnp.zeros_like(vprev_ref)
        s_ref[...] = jnp.zeros_like(s_ref)

    x = x_ref[...]
    xb = x.astype(BF16)

    def proj(lo, width=LANES):
        return _mm(xb, w_in_ref[:, lo:lo + width])

    lane = lax.broadcasted_iota(jnp.int32, (1, LANES), 1)
    lo_half = lane < HALF
    lane_in_head = lane & (HALF - 1)

    pos = pos_ref[...].astype(F32)
    ang = pos * invf_ref[...]
    cos = jnp.cos(ang)
    sin = jnp.sin(ang)
    sin_up = jnp.where(lane_in_head >= ROT_DIM // 2, sin, 0.0)
    sin_dn = jnp.where(lane_in_head < ROT_DIM // 2, -sin, 0.0)

    def rope(t):
        return (t * cos + pltpu.roll(t, ROT_DIM // 2, 1) * sin_up
                + pltpu.roll(t, LANES - ROT_DIM // 2, 1) * sin_dn)

    n_blk = TILE // BLOCK
    kv_new = proj(KA, 2 * LANES)
    k_new = rope(kv_new[:, :LANES])
    v_new = kv_new[:, LANES:]
    k_cat = jnp.concatenate([kprev_ref[...], k_new], axis=0)
    v_cat = jnp.concatenate([vprev_ref[...], v_new], axis=0)
    kprev_ref[...] = k_new[TILE - BLOCK:]
    vprev_ref[...] = v_new[TILE - BLOCK:]

    def head_copies(t):
        swapped = pltpu.roll(t, HALF, 1)
        tb, sb = t.astype(BF16), swapped.astype(BF16)
        zero = jnp.zeros_like(tb)
        lo = [jnp.where(lo_half, tb, zero), jnp.where(lo_half, sb, zero)]
        hi = [jnp.where(lo_half, zero, sb), jnp.where(lo_half, zero, tb)]
        return lo, hi

    k_lo, k_hi = head_copies(k_cat)
    v_lo, v_hi = head_copies(v_cat)

    qi = lax.broadcasted_iota(jnp.int32, (BLOCK, 2 * BLOCK), 0)
    ki = lax.broadcasted_iota(jnp.int32, (BLOCK, 2 * BLOCK), 1)
    in_window = (ki > qi) & (ki <= qi + WINDOW)
    first_blk_mask = in_window & ((ki >= BLOCK) | (t_idx > 0))

    scale = SWA_HEAD_DIM ** -0.5
    q_a = proj(QA, SWA_WIDTH)
    gate_a = jax.nn.silu(proj(GA, SWA_WIDTH))
    for cg in range(SWA_WIDTH // LANES):
        h = cg // 2
        cg_cols = slice(cg * LANES, (cg + 1) * LANES)
        q2 = (rope(q_a[:, cg_cols]) * scale).astype(BF16)
        for i in range(n_blk):
            rows = slice(i * BLOCK, (i + 1) * BLOCK)
            kv_rows = slice(i * BLOCK, (i + 2) * BLOCK)
            mask = first_blk_mask if i == 0 else in_window
            acc = None
            invs = []
            for par, (kk, vv) in enumerate(((k_lo[h], v_lo[h]), (k_hi[h], v_hi[h]))):
                sink = sinks_ref[2 * cg + par]
                s = _nt(q2[rows], kk[kv_rows])
                s = jnp.where(mask, s, -jnp.inf)
                m = jnp.maximum(jnp.max(s, axis=-1, keepdims=True), sink)
                p = jnp.exp(s - m)
                denom = jnp.sum(p, axis=-1, keepdims=True) + jnp.exp(sink - m)
                invs.append(1.0 / denom)
                pv = _mm(p.astype(BF16), vv[kv_rows])
                acc = pv if acc is None else acc + pv
            out = acc * jnp.where(lo_half, invs[0], invs[1]) * gate_a[rows, cg_cols]
            mix_ref[rows, cg_cols] = out.astype(BF16)

    n_chunk = TILE // GLA_CHUNK
    pair_w = 2 * GLA_DV
    r_b = proj(RB).astype(BF16)
    logit = _mm(r_b, w_up_ref[...]) + bg_ref[...]
    log_a = jax.nn.log_sigmoid(logit) / GLA_TAU

    ri = lax.broadcasted_iota(jnp.int32, (TILE, TILE), 0)
    ci = lax.broadcasted_iota(jnp.int32, (TILE, TILE), 1)
    tri = ((ri // GLA_CHUNK == ci // GLA_CHUNK) & (ci <= ri)).astype(BF16)
    la_hi = log_a.astype(BF16)
    la_lo = (log_a - la_hi.astype(F32)).astype(BF16)
    b = _mm(tri, la_hi) + _mm(tri, la_lo)

    qk_b = proj(QB, 2 * GLA_HEADS * GLA_DK)
    q_b = qk_b[:, :GLA_HEADS * GLA_DK]
    k_b = qk_b[:, GLA_HEADS * GLA_DK:]
    v_b = proj(VB, GLA_WIDTH).astype(BF16)
    gate_b = jax.nn.silu(proj(GB, GLA_WIDTH))
    q_dec = (q_b * (GLA_DK ** -0.5) * jnp.exp(b)).astype(BF16)
    k_inv = (k_b * jnp.exp(-b)).astype(BF16)

    causal2 = (lax.broadcasted_iota(jnp.int32, (GLA_CHUNK, LANES), 1) & (HALF - 1)
               ) <= lax.broadcasted_iota(jnp.int32, (GLA_CHUNK, LANES), 0)
    eye = (lax.broadcasted_iota(jnp.int32, (LANES, LANES), 0)
           == lax.broadcasted_iota(jnp.int32, (LANES, LANES), 1))
    norm_w = nw_ref[...]
    zero_v = jnp.zeros((GLA_CHUNK, GLA_DV), BF16)

    for pair in range(GLA_HEADS // 2):
        cols = slice(pair * LANES, (pair + 1) * LANES)
        vcols = slice(pair * pair_w, (pair + 1) * pair_w)
        state = s_ref[cols, :]
        for c in range(n_chunk):
            rows = slice(c * GLA_CHUNK, (c + 1) * GLA_CHUNK)
            last = (c + 1) * GLA_CHUNK - 1
            b_c = b[rows, cols]
            b_last = b[last:last + 1, cols]
            q2 = q_dec[rows, cols]
            ki2 = k_inv[rows, cols]
            v2 = v_b[rows, vcols]
            k_end = (k_b[rows, cols] * jnp.exp(b_last - b_c)).astype(BF16)
            zero = jnp.zeros_like(ki2)
            k_stack = jnp.concatenate([jnp.where(lo_half, ki2, zero),
                                       jnp.where(lo_half, zero, ki2)], axis=0)
            attn = jnp.where(causal2, _nt(q2, k_stack), 0.0).astype(BF16)
            state_b = state.astype(BF16)
            rhs = jnp.concatenate([
                jnp.concatenate([v2[:, :GLA_DV], zero_v], axis=1),
                jnp.concatenate([zero_v, v2[:, GLA_DV:]], axis=1),
                jnp.concatenate([state_b[:GLA_DK], zero_v], axis=1),
                jnp.concatenate([zero_v, state_b[GLA_DK:]], axis=1)], axis=0)
            o2 = _mm(jnp.concatenate([attn, q2], axis=1), rhs)
            for par in range(2):
                o = o2[:, par * GLA_DV:(par + 1) * GLA_DV]
                o = o * lax.rsqrt(jnp.mean(jnp.square(o), axis=-1, keepdims=True) + EPS) * norm_w
                c0 = (2 * pair + par) * GLA_DV
                out = o * gate_b[rows, c0:c0 + GLA_DV]
                mix_ref[rows, SWA_WIDTH + c0:SWA_WIDTH + c0 + GLA_DV] = out.astype(BF16)
            upd = _tn(k_end, v2)
            upd = jnp.concatenate([upd[:GLA_DK, :GLA_DV], upd[GLA_DK:, GLA_DV:]], axis=0)
            decay_row = jnp.broadcast_to(jnp.exp(b_last), (LANES, LANES))
            decay_col = jnp.sum(jnp.where(eye, decay_row, 0.0), axis=1, keepdims=True)
            state = state * decay_col + upd
        s_ref[cols, :] = state

    y = _mm(mix_ref[...], w_out_ref[...]) + ALPHA * x
    mu = jnp.mean(y, axis=-1, keepdims=True)
    yc = y - mu
    var = jnp.mean(jnp.square(yc), axis=-1, keepdims=True)
    o_ref[...] = yc * lax.rsqrt(var + EPS) * lng_ref[...] + lnb_ref[...]


def _rope_lane_freqs():
    half = ROT_DIM // 2
    inv_freq = ROPE_THETA ** (-jnp.arange(half, dtype=F32) / half)
    lane = np.arange(LANES)
    in_head = lane % HALF
    table = jnp.where(in_head < ROT_DIM, inv_freq[in_head % half], 0.0)
    return table.reshape(1, LANES).astype(F32)


@jax.jit
def kernel(x, positions, w_in, gla_w_gate_up, gla_b_gate, attn_sinks, gla_norm_w, w_out, ln_g, ln_b):
    B, S, D = x.shape
    assert D == D_MODEL and S % TILE == 0 and w_in.shape == (DEPTH, D_MODEL, D_IN_PROJ)
    w_in_p = jnp.pad(w_in[0].astype(BF16), ((0, 0), (0, D_IN_PAD - D_IN_PROJ)))
    w_up_p = jnp.pad(gla_w_gate_up[0].astype(BF16), ((0, LANES - GLA_RANK), (0, 0)))
    pos3 = positions.reshape(B, S, 1)

    const = lambda shape: pl.BlockSpec(shape, lambda b, t: (0,) * len(shape))
    grid_spec = pl.GridSpec(
        grid=(B, S // TILE),
        in_specs=[
            pl.BlockSpec((None, TILE, D_MODEL), lambda b, t: (b, t, 0)),
            pl.BlockSpec((None, TILE, 1), lambda b, t: (b, t, 0)),
            const((D_MODEL, D_IN_PAD)),
            const((LANES, GLA_HEADS * GLA_DK)),
            const((1, GLA_HEADS * GLA_DK)),
            const((1, GLA_DV)),
            const((D_MODEL, D_MODEL)),
            const((1, D_MODEL)),
            const((1, D_MODEL)),
            const((1, LANES)),
            pl.BlockSpec(memory_space=pltpu.SMEM),
        ],
        out_specs=pl.BlockSpec((None, TILE, D_MODEL), lambda b, t: (b, t, 0)),
        scratch_shapes=[
            pltpu.VMEM((BLOCK, LANES), F32),
            pltpu.VMEM((BLOCK, LANES), F32),
            pltpu.VMEM((GLA_HEADS * GLA_DK, GLA_DV), F32),
            pltpu.VMEM((TILE, D_MODEL), BF16),
        ],
    )
    return pl.pallas_call(
        _layer_kernel,
        grid_spec=grid_spec,
        out_shape=jax.ShapeDtypeStruct((B, S, D_MODEL), x.dtype),
        compiler_params=pltpu.CompilerParams(
            dimension_semantics=("arbitrary", "arbitrary"),
            vmem_limit_bytes=VMEM_LIMIT_BYTES),
        name="hymba_layer",
    )(x, pos3, w_in_p, w_up_p, gla_b_gate[0].reshape(1, -1), gla_norm_w[0].reshape(1, -1),
      w_out[0].astype(BF16), ln_g[0].reshape(1, -1), ln_b[0].reshape(1, -1),
      _rope_lane_freqs(), attn_sinks[0])
```

```python
import numpy as np
import jax
import jax.numpy as jnp
from jax import lax
from jax.experimental import pallas as pl
from jax.experimental.pallas import tpu as pltpu

D_MODEL = 1024
SWA_WIDTH = 512
SWA_HEAD_DIM = 64
SWA_Q_HEADS = 8
SWA_KV_HEADS = 2
WINDOW = 128
BLOCK = 128
ROPE_THETA = 500000.0
ROT_DIM = 16
GLA_WIDTH = 512
GLA_HEADS = 4
GLA_DK = 64
GLA_DV = 128
GLA_RANK = 16
GLA_TAU = 16.0
GLA_CHUNK = 64
EPS = 1e-5
DEPTH = 1
ALPHA = (2 * DEPTH) ** 0.25

LANES = 128
HALF = LANES // 2

QA, KA, VA, GA, QB, KB, VB, GB, RB = 0, 512, 640, 768, 1280, 1536, 1792, 2304, 2816
D_IN_PROJ = RB + GLA_RANK
D_IN_PAD = RB + LANES

TILE = 256
VMEM_LIMIT_BYTES = 48 * 1024 * 1024

F32 = jnp.float32
BF16 = jnp.bfloat16


def _nt(a, b):
    return lax.dot_general(a, b, (((1,), (1,)), ((), ())), preferred_element_type=F32)


def _tn(a, b):
    return lax.dot_general(a, b, (((0,), (0,)), ((), ())), preferred_element_type=F32)


def _mm(a, b):
    return jnp.dot(a, b, preferred_element_type=F32)


def _layer_kernel(x_ref, pos_ref, w_in_ref, w_up_ref, bg_ref, nw_ref, w_out_ref,
                  lng_ref, lnb_ref, invf_ref, sinks_ref,
                  o_ref, kprev_ref, vprev_ref, s_ref, mix_ref):
    t_idx = pl.program_id(1)

    @pl.when(t_idx == 0)
    def _reset_carries():
        kprev_ref[...] = jnp.zeros_like(kprev_ref)
        vprev_ref[...] = jnp.zeros_like(vprev_ref)
        s_ref[...] = jnp.zeros_like(s_ref)

    x = x_ref[...]
    xb = x.astype(BF16)

    def proj(lo, width=LANES):
        return _mm(xb, w_in_ref[:, lo:lo + width])

    lane = lax.broadcasted_iota(jnp.int32, (1, LANES), 1)
    lo_half = lane < HALF
    lane_in_head = lane & (HALF - 1)

    pos = pos_ref[...].astype(F32)
    ang = pos * invf_ref[...]
    cos = jnp.cos(ang)
    sin = jnp.sin(ang)
    sin_up = jnp.where(lane_in_head >= ROT_DIM // 2, sin, 0.0)
    sin_dn = jnp.where(lane_in_head < ROT_DIM // 2, -sin, 0.0)

    def rope(t):
        return (t * cos + pltpu.roll(t, ROT_DIM // 2, 1) * sin_up
                + pltpu.roll(t, LANES - ROT_DIM // 2, 1) * sin_dn)

    n_blk = TILE // BLOCK
    kv_new = proj(KA, 2 * LANES)
    k_new = rope(kv_new[:, :LANES])
    v_new = kv_new[:, LANES:]
    k_cat = jnp.concatenate([kprev_ref[...], k_new], axis=0)
    v_cat = jnp.concatenate([vprev_ref[...], v_new], axis=0)
    kprev_ref[...] = k_new[TILE - BLOCK:]
    vprev_ref[...] = v_new[TILE - BLOCK:]

    k_sw = pltpu.roll(k_cat, HALF, 1)
    k_cb, k_sb = k_cat.astype(BF16), k_sw.astype(BF16)
    k_zero = jnp.zeros_like(k_cb)
    k_even = [jnp.where(lo_half, k_cb, k_zero), jnp.where(lo_half, k_sb, k_zero)]
    k_odd = [jnp.where(lo_half, k_zero, k_sb), jnp.where(lo_half, k_zero, k_cb)]
    v_t = v_cat.T.astype(BF16)

    kj = lax.broadcasted_iota(jnp.int32, (2 * BLOCK, BLOCK), 0)
    qj = lax.broadcasted_iota(jnp.int32, (2 * BLOCK, BLOCK), 1)
    in_window = (kj > qj) & (kj <= qj + WINDOW)
    first_blk_mask = in_window & ((kj >= BLOCK) | (t_idx > 0))

    scale = SWA_HEAD_DIM ** -0.5
    q_a = proj(QA, SWA_WIDTH)
    gate_a = jax.nn.silu(proj(GA, SWA_WIDTH))
    q_r = [(rope(q_a[:, cg * LANES:(cg + 1) * LANES]) * scale).astype(BF16)
           for cg in range(SWA_WIDTH // LANES)]
    for h in range(SWA_KV_HEADS):
        for i in range(n_blk):
            rows = slice(i * BLOCK, (i + 1) * BLOCK)
            kv_rows = slice(i * BLOCK, (i + 2) * BLOCK)
            mask = first_blk_mask if i == 0 else in_window
            q_stack = jnp.concatenate([q_r[2 * h][rows], q_r[2 * h + 1][rows]], axis=0)
            s_par = [_nt(k_even[h][kv_rows], q_stack),
                     _nt(k_odd[h][kv_rows], q_stack)]
            p_t, inv = [], []
            for g in range(4):
                sink = sinks_ref[4 * h + g]
                s = s_par[g % 2][:, (g // 2) * BLOCK:(g // 2 + 1) * BLOCK]
                s = jnp.where(mask, s, -jnp.inf)
                m = jnp.maximum(jnp.max(s, axis=0, keepdims=True), sink)
                p = jnp.exp(s - m)
                denom = jnp.sum(p, axis=0, keepdims=True) + jnp.exp(sink - m)
                inv.append(1.0 / denom)
                p_t.append(p.astype(BF16))
            o_t = _mm(v_t[h * HALF:(h + 1) * HALF, kv_rows], jnp.concatenate(p_t, axis=1))
            for c in range(2):
                cg = 2 * h + c
                o2_t = jnp.concatenate(
                    [o_t[:, (2 * c + par) * BLOCK:(2 * c + par + 1) * BLOCK] * inv[2 * c + par]
                     for par in range(2)], axis=0)
                cg_cols = slice(cg * LANES, (cg + 1) * LANES)
                out = o2_t.T * gate_a[rows, cg_cols]
                mix_ref[rows, cg_cols] = out.astype(BF16)

    n_chunk = TILE // GLA_CHUNK
    pair_w = 2 * GLA_DV
    r_b = proj(RB).astype(BF16)
    logit = _mm(r_b, w_up_ref[...]) + bg_ref[...]
    log_a = jax.nn.log_sigmoid(logit) / GLA_TAU

    ri = lax.broadcasted_iota(jnp.int32, (TILE, TILE), 0)
    ci = lax.broadcasted_iota(jnp.int32, (TILE, TILE), 1)
    tri = ((ri // GLA_CHUNK == ci // GLA_CHUNK) & (ci <= ri)).astype(BF16)
    la_hi = log_a.astype(BF16)
    la_lo = (log_a - la_hi.astype(F32)).astype(BF16)
    b = _mm(tri, la_hi) + _mm(tri, la_lo)

    qk_b = proj(QB, 2 * GLA_HEADS * GLA_DK)
    q_b = qk_b[:, :GLA_HEADS * GLA_DK]
    k_b = qk_b[:, GLA_HEADS * GLA_DK:]
    v_b = proj(VB, GLA_WIDTH).astype(BF16)
    gate_b = jax.nn.silu(proj(GB, GLA_WIDTH))
    q_dec = (q_b * (GLA_DK ** -0.5) * jnp.exp(b)).astype(BF16)
    k_inv = (k_b * jnp.exp(-b)).astype(BF16)

    causal2 = (lax.broadcasted_iota(jnp.int32, (GLA_CHUNK, LANES), 1) & (HALF - 1)
               ) <= lax.broadcasted_iota(jnp.int32, (GLA_CHUNK, LANES), 0)
    eye = (lax.broadcasted_iota(jnp.int32, (LANES, LANES), 0)
           == lax.broadcasted_iota(jnp.int32, (LANES, LANES), 1))
    norm_w = nw_ref[...]
    zero_v = jnp.zeros((GLA_CHUNK, GLA_DV), BF16)

    for pair in range(GLA_HEADS // 2):
        cols = slice(pair * LANES, (pair + 1) * LANES)
        vcols = slice(pair * pair_w, (pair + 1) * pair_w)
        state = s_ref[cols, :]
        for c in range(n_chunk):
            rows = slice(c * GLA_CHUNK, (c + 1) * GLA_CHUNK)
            last = (c + 1) * GLA_CHUNK - 1
            b_c = b[rows, cols]
            b_last = b[last:last + 1, cols]
            q2 = q_dec[rows, cols]
            ki2 = k_inv[rows, cols]
            v2 = v_b[rows, vcols]
            k_end = (k_b[rows, cols] * jnp.exp(b_last - b_c)).astype(BF16)
            zero = jnp.zeros_like(ki2)
            k_stack = jnp.concatenate([jnp.where(lo_half, ki2, zero),
                                       jnp.where(lo_half, zero, ki2)], axis=0)
            attn = jnp.where(causal2, _nt(q2, k_stack), 0.0).astype(BF16)
            state_b = state.astype(BF16)
            rhs = jnp.concatenate([
                jnp.concatenate([v2[:, :GLA_DV], zero_v], axis=1),
                jnp.concatenate([zero_v, v2[:, GLA_DV:]], axis=1),
                jnp.concatenate([state_b[:GLA_DK], zero_v], axis=1),
                jnp.concatenate([zero_v, state_b[GLA_DK:]], axis=1)], axis=0)
            o2 = _mm(jnp.concatenate([attn, q2], axis=1), rhs)
            for par in range(2):
                o = o2[:, par * GLA_DV:(par + 1) * GLA_DV]
                o = o * lax.rsqrt(jnp.mean(jnp.square(o), axis=-1, keepdims=True) + EPS) * norm_w
                c0 = (2 * pair + par) * GLA_DV
                out = o * gate_b[rows, c0:c0 + GLA_DV]
                mix_ref[rows, SWA_WIDTH + c0:SWA_WIDTH + c0 + GLA_DV] = out.astype(BF16)
            upd = _tn(k_end, v2)
            upd = jnp.concatenate([upd[:GLA_DK, :GLA_DV], upd[GLA_DK:, GLA_DV:]], axis=0)
            decay_row = jnp.broadcast_to(jnp.exp(b_last), (LANES, LANES))
            decay_col = jnp.sum(jnp.where(eye, decay_row, 0.0), axis=1, keepdims=True)
            state = state * decay_col + upd
        s_ref[cols, :] = state

    y = _mm(mix_ref[...], w_out_ref[...]) + ALPHA * x
    mu = jnp.mean(y, axis=-1, keepdims=True)
    yc = y - mu
    var = jnp.mean(jnp.square(yc), axis=-1, keepdims=True)
    o_ref[...] = yc * lax.rsqrt(var + EPS) * lng_ref[...] + lnb_ref[...]


def _rope_lane_freqs():
    half = ROT_DIM // 2
    inv_freq = ROPE_THETA ** (-jnp.arange(half, dtype=F32) / half)
    lane = np.arange(LANES)
    in_head = lane % HALF
    table = jnp.where(in_head < ROT_DIM, inv_freq[in_head % half], 0.0)
    return table.reshape(1, LANES).astype(F32)


@jax.jit
def kernel(x, positions, w_in, gla_w_gate_up, gla_b_gate, attn_sinks, gla_norm_w, w_out, ln_g, ln_b):
    B, S, D = x.shape
    assert D == D_MODEL and S % TILE == 0 and w_in.shape == (DEPTH, D_MODEL, D_IN_PROJ)
    w_in_p = jnp.pad(w_in[0].astype(BF16), ((0, 0), (0, D_IN_PAD - D_IN_PROJ)))
    w_up_p = jnp.pad(gla_w_gate_up[0].astype(BF16), ((0, LANES - GLA_RANK), (0, 0)))
    pos3 = positions.reshape(B, S, 1)

    const = lambda shape: pl.BlockSpec(shape, lambda b, t: (0,) * len(shape))
    grid_spec = pl.GridSpec(
        grid=(B, S // TILE),
        in_specs=[
            pl.BlockSpec((None, TILE, D_MODEL), lambda b, t: (b, t, 0)),
            pl.BlockSpec((None, TILE, 1), lambda b, t: (b, t, 0)),
            const((D_MODEL, D_IN_PAD)),
            const((LANES, GLA_HEADS * GLA_DK)),
            const((1, GLA_HEADS * GLA_DK)),
            const((1, GLA_DV)),
            const((D_MODEL, D_MODEL)),
            const((1, D_MODEL)),
            const((1, D_MODEL)),
            const((1, LANES)),
            pl.BlockSpec(memory_space=pltpu.SMEM),
        ],
        out_specs=pl.BlockSpec((None, TILE, D_MODEL), lambda b, t: (b, t, 0)),
        scratch_shapes=[
            pltpu.VMEM((BLOCK, LANES), F32),
            pltpu.VMEM((BLOCK, LANES), F32),
            pltpu.VMEM((GLA_HEADS * GLA_DK, GLA_DV), F32),
            pltpu.VMEM((TILE, D_MODEL), BF16),
        ],
    )
    return pl.pallas_call(
        _layer_kernel,
        grid_spec=grid_spec,
        out_shape=jax.ShapeDtypeStruct((B, S, D_MODEL), x.dtype),
        compiler_params=pltpu.CompilerParams(
            dimension_semantics=("arbitrary", "arbitrary"),
            vmem_limit_bytes=VMEM_LIMIT_BYTES),
        name="hymba_layer",
    )(x, pos3, w_in_p, w_up_p, gla_b_gate[0].reshape(1, -1), gla_norm_w[0].reshape(1, -1),
      w_out[0].astype(BF16), ln_g[0].reshape(1, -1), ln_b[0].reshape(1, -1),
      _rope_lane_freqs(), attn_sinks[0])
```

```python
import numpy as np
import jax
import jax.numpy as jnp
from jax import lax
from jax.experimental import pallas as pl
from jax.experimental.pallas import tpu as pltpu

D_MODEL = 1024
SWA_WIDTH = 512
SWA_HEAD_DIM = 64
SWA_Q_HEADS = 8
SWA_KV_HEADS = 2
WINDOW = 128
BLOCK = 128
ROPE_THETA = 500000.0
ROT_DIM = 16
GLA_WIDTH = 512
GLA_HEADS = 4
GLA_DK = 64
GLA_DV = 128
GLA_RANK = 16
GLA_TAU = 16.0
GLA_CHUNK = 64
EPS = 1e-5
DEPTH = 1
ALPHA = (2 * DEPTH) ** 0.25

LANES = 128
HALF = LANES // 2

QA, KA, VA, GA, QB, KB, VB, GB, RB = 0, 512, 640, 768, 1280, 1536, 1792, 2304, 2816
D_IN_PROJ = RB + GLA_RANK
D_IN_PAD = RB + LANES

TILE = 256
VMEM_LIMIT_BYTES = 48 * 1024 * 1024

F32 = jnp.float32
BF16 = jnp.bfloat16


def _nt(a, b):
    return lax.dot_general(a, b, (((1,), (1,)), ((), ())), preferred_element_type=F32)


def _tn(a, b):
    return lax.dot_general(a, b, (((0,), (0,)), ((), ())), preferred_element_type=F32)


def _mm(a, b):
    return jnp.dot(a, b, preferred_element_type=F32)


def _layer_kernel(x_ref, pos_ref, w_in_ref, w_up_ref, bg_ref, nw_ref, w_out_ref,
                  lng_ref, lnb_ref, invf_ref, sinks_ref,
                  o_ref, kprev_ref, vprev_ref, s_ref, mix_ref):
    t_idx = pl.program_id(1)

    @pl.when(t_idx == 0)
    def _reset_carries():
        kprev_ref[...] = jnp.zeros_like(kprev_ref)
        vprev_ref[...] = jnp.zeros_like(vprev_ref)
        s_ref[...] = jnp.zeros_like(s_ref)

    x = x_ref[...]
    xb = x.astype(BF16)

    def proj(lo, width=LANES):
        return _mm(xb, w_in_ref[:, lo:lo + width])

    lane = lax.broadcasted_iota(jnp.int32, (1, LANES), 1)
    lo_half = lane < HALF
    lane_in_head = lane & (HALF - 1)
    n_blk = TILE // BLOCK
    n_chunk = TILE // GLA_CHUNK
    n_pair = GLA_HEADS // 2
    pair_w = 2 * GLA_DV

    r_b = proj(RB).astype(BF16)
    kv_new = proj(KA, 2 * LANES)
    q_a = proj(QA, SWA_WIDTH)
    logit = _mm(r_b, w_up_ref[...]) + bg_ref[...]
    log_a = jax.nn.log_sigmoid(logit) / GLA_TAU
    qk_b = proj(QB, 2 * GLA_HEADS * GLA_DK)
    q_b = qk_b[:, :GLA_HEADS * GLA_DK]
    k_b = qk_b[:, GLA_HEADS * GLA_DK:]

    ri = lax.broadcasted_iota(jnp.int32, (TILE, TILE), 0)
    ci = lax.broadcasted_iota(jnp.int32, (TILE, TILE), 1)
    tri = ((ri // GLA_CHUNK == ci // GLA_CHUNK) & (ci <= ri)).astype(BF16)
    la_hi = log_a.astype(BF16)
    la_lo = (log_a - la_hi.astype(F32)).astype(BF16)
    b = _mm(tri, la_hi) + _mm(tri, la_lo)
    v_b = proj(VB, GLA_WIDTH).astype(BF16)

    pos = pos_ref[...].astype(F32)
    ang = pos * invf_ref[...]
    cos = jnp.cos(ang)
    sin = jnp.sin(ang)
    sin_up = jnp.where(lane_in_head >= ROT_DIM // 2, sin, 0.0)
    sin_dn = jnp.where(lane_in_head < ROT_DIM // 2, -sin, 0.0)

    def rope(t):
        return (t * cos + pltpu.roll(t, ROT_DIM // 2, 1) * sin_up
                + pltpu.roll(t, LANES - ROT_DIM // 2, 1) * sin_dn)

    k_new = rope(kv_new[:, :LANES])
    v_new = kv_new[:, LANES:]
    k_cat = jnp.concatenate([kprev_ref[...], k_new], axis=0)
    v_cat = jnp.concatenate([vprev_ref[...], v_new], axis=0)
    kprev_ref[...] = k_new[TILE - BLOCK:]
    vprev_ref[...] = v_new[TILE - BLOCK:]

    k_sw = pltpu.roll(k_cat, HALF, 1)
    k_cb, k_sb = k_cat.astype(BF16), k_sw.astype(BF16)
    k_zero = jnp.zeros_like(k_cb)
    k_even = [jnp.where(lo_half, k_cb, k_zero), jnp.where(lo_half, k_sb, k_zero)]
    k_odd = [jnp.where(lo_half, k_zero, k_sb), jnp.where(lo_half, k_zero, k_cb)]
    v_t = v_cat.T.astype(BF16)

    scale = SWA_HEAD_DIM ** -0.5
    q_r = [(rope(q_a[:, cg * LANES:(cg + 1) * LANES]) * scale).astype(BF16)
           for cg in range(SWA_WIDTH // LANES)]

    swa_tiles = [(h, i) for h in range(SWA_KV_HEADS) for i in range(n_blk)]
    scores = {}
    for h, i in swa_tiles:
        rows = slice(i * BLOCK, (i + 1) * BLOCK)
        kv_rows = slice(i * BLOCK, (i + 2) * BLOCK)
        q_stack = jnp.concatenate([q_r[2 * h][rows], q_r[2 * h + 1][rows]], axis=0)
        scores[h, i] = [_nt(k_even[h][kv_rows], q_stack),
                        _nt(k_odd[h][kv_rows], q_stack)]

    q_dec = (q_b * (GLA_DK ** -0.5) * jnp.exp(b)).astype(BF16)
    k_inv = (k_b * jnp.exp(-b)).astype(BF16)
    causal2 = (lax.broadcasted_iota(jnp.int32, (GLA_CHUNK, LANES), 1) & (HALF - 1)
               ) <= lax.broadcasted_iota(jnp.int32, (GLA_CHUNK, LANES), 0)
    eye = (lax.broadcasted_iota(jnp.int32, (LANES, LANES), 0)
           == lax.broadcasted_iota(jnp.int32, (LANES, LANES), 1))
    gla_tiles = [(p, c) for p in range(n_pair) for c in range(n_chunk)]
    attn, upd, decay_col = {}, {}, {}
    for p, c in gla_tiles:
        cols = slice(p * LANES, (p + 1) * LANES)
        rows = slice(c * GLA_CHUNK, (c + 1) * GLA_CHUNK)
        last = (c + 1) * GLA_CHUNK - 1
        b_last = b[last:last + 1, cols]
        ki2 = k_inv[rows, cols]
        k_end = (k_b[rows, cols] * jnp.exp(b_last - b[rows, cols])).astype(BF16)
        zero = jnp.zeros_like(ki2)
        k_stack = jnp.concatenate([jnp.where(lo_half, ki2, zero),
                                   jnp.where(lo_half, zero, ki2)], axis=0)
        attn[p, c] = jnp.where(causal2, _nt(q_dec[rows, cols], k_stack), 0.0).astype(BF16)
        u = _tn(k_end, v_b[rows, p * pair_w:(p + 1) * pair_w])
        upd[p, c] = jnp.concatenate([u[:GLA_DK, :GLA_DV], u[GLA_DK:, GLA_DV:]], axis=0)
        decay_row = jnp.broadcast_to(jnp.exp(b_last), (LANES, LANES))
        decay_col[p, c] = jnp.sum(jnp.where(eye, decay_row, 0.0), axis=1, keepdims=True)

    gate_a = jax.nn.silu(proj(GA, SWA_WIDTH))
    gate_b = jax.nn.silu(proj(GB, GLA_WIDTH))

    kj = lax.broadcasted_iota(jnp.int32, (2 * BLOCK, BLOCK), 0)
    qj = lax.broadcasted_iota(jnp.int32, (2 * BLOCK, BLOCK), 1)
    in_window = (kj > qj) & (kj <= qj + WINDOW)
    first_blk_mask = in_window & ((kj >= BLOCK) | (t_idx > 0))
    probs, inv_den = {}, {}
    for h, i in swa_tiles:
        mask = first_blk_mask if i == 0 else in_window
        p_t, inv = [], []
        for g in range(4):
            sink = sinks_ref[4 * h + g]
            s = scores[h, i][g % 2][:, (g // 2) * BLOCK:(g // 2 + 1) * BLOCK]
            s = jnp.where(mask, s, -jnp.inf)
            m = jnp.maximum(jnp.max(s, axis=0, keepdims=True), sink)
            e = jnp.exp(s - m)
            denom = jnp.sum(e, axis=0, keepdims=True) + jnp.exp(sink - m)
            inv.append(1.0 / denom)
            p_t.append(e.astype(BF16))
        probs[h, i] = jnp.concatenate(p_t, axis=1)
        inv_den[h, i] = inv

    states = {}
    for p in range(n_pair):
        state = s_ref[p * LANES:(p + 1) * LANES, :]
        for c in range(n_chunk):
            states[p, c] = state.astype(BF16)
            state = state * decay_col[p, c] + upd[p, c]
        s_ref[p * LANES:(p + 1) * LANES, :] = state

    o_swa = {}
    for h, i in swa_tiles:
        kv_rows = slice(i * BLOCK, (i + 2) * BLOCK)
        o_swa[h, i] = _mm(v_t[h * HALF:(h + 1) * HALF, kv_rows], probs[h, i])
    zero_v = jnp.zeros((GLA_CHUNK, GLA_DV), BF16)
    o_gla = {}
    for p, c in gla_tiles:
        rows = slice(c * GLA_CHUNK, (c + 1) * GLA_CHUNK)
        v2 = v_b[rows, p * pair_w:(p + 1) * pair_w]
        st = states[p, c]
        rhs = jnp.concatenate([
            jnp.concatenate([v2[:, :GLA_DV], zero_v], axis=1),
            jnp.concatenate([zero_v, v2[:, GLA_DV:]], axis=1),
            jnp.concatenate([st[:GLA_DK], zero_v], axis=1),
            jnp.concatenate([zero_v, st[GLA_DK:]], axis=1)], axis=0)
        lhs = jnp.concatenate([attn[p, c], q_dec[rows, p * LANES:(p + 1) * LANES]], axis=1)
        o_gla[p, c] = _mm(lhs, rhs)

    for h, i in swa_tiles:
        rows = slice(i * BLOCK, (i + 1) * BLOCK)
        o_t, inv = o_swa[h, i], inv_den[h, i]
        for c in range(2):
            cg = 2 * h + c
            o2_t = jnp.concatenate(
                [o_t[:, (2 * c + par) * BLOCK:(2 * c + par + 1) * BLOCK] * inv[2 * c + par]
                 for par in range(2)], axis=0)
            cg_cols = slice(cg * LANES, (cg + 1) * LANES)
            out = o2_t.T * gate_a[rows, cg_cols]
            mix_ref[rows, cg_cols] = out.astype(BF16)
    norm_w = nw_ref[...]
    for p, c in gla_tiles:
        rows = slice(c * GLA_CHUNK, (c + 1) * GLA_CHUNK)
        for par in range(2):
            o = o_gla[p, c][:, par * GLA_DV:(par + 1) * GLA_DV]
            o = o * lax.rsqrt(jnp.mean(jnp.square(o), axis=-1, keepdims=True) + EPS) * norm_w
            c0 = (2 * p + par) * GLA_DV
            out = o * gate_b[rows, c0:c0 + GLA_DV]
            mix_ref[rows, SWA_WIDTH + c0:SWA_WIDTH + c0 + GLA_DV] = out.astype(BF16)

    y = _mm(mix_ref[...], w_out_ref[...]) + ALPHA * x
    mu = jnp.mean(y, axis=-1, keepdims=True)
    yc = y - mu
    var = jnp.mean(jnp.square(yc), axis=-1, keepdims=True)
    o_ref[...] = yc * lax.rsqrt(var + EPS) * lng_ref[...] + lnb_ref[...]


def _rope_lane_freqs():
    half = ROT_DIM // 2
    inv_freq = ROPE_THETA ** (-jnp.arange(half, dtype=F32) / half)
    lane = np.arange(LANES)
    in_head = lane % HALF
    table = jnp.where(in_head < ROT_DIM, inv_freq[in_head % half], 0.0)
    return table.reshape(1, LANES).astype(F32)


@jax.jit
def kernel(x, positions, w_in, gla_w_gate_up, gla_b_gate, attn_sinks, gla_norm_w, w_out, ln_g, ln_b):
    B, S, D = x.shape
    assert D == D_MODEL and S % TILE == 0 and w_in.shape == (DEPTH, D_MODEL, D_IN_PROJ)
    w_in_p = jnp.pad(w_in[0].astype(BF16), ((0, 0), (0, D_IN_PAD - D_IN_PROJ)))
    w_up_p = jnp.pad(gla_w_gate_up[0].astype(BF16), ((0, LANES - GLA_RANK), (0, 0)))
    pos3 = positions.reshape(B, S, 1)

    const = lambda shape: pl.BlockSpec(shape, lambda b, t: (0,) * len(shape))
    grid_spec = pl.GridSpec(
        grid=(B, S // TILE),
        in_specs=[
            pl.BlockSpec((None, TILE, D_MODEL), lambda b, t: (b, t, 0)),
            pl.BlockSpec((None, TILE, 1), lambda b, t: (b, t, 0)),
            const((D_MODEL, D_IN_PAD)),
            const((LANES, GLA_HEADS * GLA_DK)),
            const((1, GLA_HEADS * GLA_DK)),
            const((1, GLA_DV)),
            const((D_MODEL, D_MODEL)),
            const((1, D_MODEL)),
            const((1, D_MODEL)),
            const((1, LANES)),
            pl.BlockSpec(memory_space=pltpu.SMEM),
        ],
        out_specs=pl.BlockSpec((None, TILE, D_MODEL), lambda b, t: (b, t, 0)),
        scratch_shapes=[
            pltpu.VMEM((BLOCK, LANES), F32),
            pltpu.VMEM((BLOCK, LANES), F32),
            pltpu.VMEM((GLA_HEADS * GLA_DK, GLA_DV), F32),
            pltpu.VMEM((TILE, D_MODEL), BF16),
        ],
    )
    return pl.pallas_call(
        _layer_kernel,
        grid_spec=grid_spec,
        out_shape=jax.ShapeDtypeStruct((B, S, D_MODEL), x.dtype),
        compiler_params=pltpu.CompilerParams(
            dimension_semantics=("arbitrary", "arbitrary"),
            vmem_limit_bytes=VMEM_LIMIT_BYTES),
        name="hymba_layer",
    )(x, pos3, w_in_p, w_up_p, gla_b_gate[0].reshape(1, -1), gla_norm_w[0].reshape(1, -1),
      w_out[0].astype(BF16), ln_g[0].reshape(1, -1), ln_b[0].reshape(1, -1),
      _rope_lane_freqs(), attn_sinks[0])
```

```python
import numpy as np
import jax
import jax.numpy as jnp
from jax import lax
from jax.experimental import pallas as pl
from jax.experimental.pallas import tpu as pltpu

D_MODEL = 1024
SWA_WIDTH = 512
SWA_HEAD_DIM = 64
SWA_Q_HEADS = 8
SWA_KV_HEADS = 2
WINDOW = 128
BLOCK = 128
ROPE_THETA = 500000.0
ROT_DIM = 16
GLA_WIDTH = 512
GLA_HEADS = 4
GLA_DK = 64
GLA_DV = 128
GLA_RANK = 16
GLA_TAU = 16.0
GLA_CHUNK = 64
EPS = 1e-5
DEPTH = 1
ALPHA = (2 * DEPTH) ** 0.25

LANES = 128
HALF = LANES // 2

QA, KA, VA, GA, QB, KB, VB, GB, RB = 0, 512, 640, 768, 1280, 1536, 1792, 2304, 2816
D_IN_PROJ = RB + GLA_RANK

TILE = 256
W_CAST_ROWS = 128
VMEM_LIMIT_BYTES = 56 * 1024 * 1024

F32 = jnp.float32
BF16 = jnp.bfloat16


def _nt(a, b):
    return lax.dot_general(a, b, (((1,), (1,)), ((), ())), preferred_element_type=F32)


def _tn(a, b):
    return lax.dot_general(a, b, (((0,), (0,)), ((), ())), preferred_element_type=F32)


def _mm(a, b):
    return jnp.dot(a, b, preferred_element_type=F32)


def _layer_kernel(x_ref, pos_ref, w_in_f32, w_up_f32, bg_ref, nw_ref, w_out_f32,
                  lng_ref, lnb_ref, invf_ref, sinks_ref,
                  o_ref, kprev_ref, vprev_ref, s_ref, mix_ref,
                  w_in_ref, w_r_ref, w_up_ref, w_out_ref):
    t_idx = pl.program_id(1)

    @pl.when((pl.program_id(0) == 0) & (t_idx == 0))
    def _cast_weights():
        w_r_ref[...] = jnp.zeros_like(w_r_ref)

        def cast_rows(r, carry):
            rows = pl.ds(pl.multiple_of(r * W_CAST_ROWS, W_CAST_ROWS), W_CAST_ROWS)
            w_in_ref[rows, :] = w_in_f32[rows, :RB].astype(BF16)
            w_r_ref[rows, :GLA_RANK] = w_in_f32[rows, RB:].astype(BF16)
            w_out_ref[rows, :] = w_out_f32[rows, :].astype(BF16)
            return carry
        lax.fori_loop(0, D_MODEL // W_CAST_ROWS, cast_rows, 0)
        w_up_ref[...] = jnp.concatenate(
            [w_up_f32[...], jnp.zeros((LANES - GLA_RANK, GLA_HEADS * GLA_DK), F32)],
            axis=0).astype(BF16)

    @pl.when(t_idx == 0)
    def _reset_carries():
        kprev_ref[...] = jnp.zeros_like(kprev_ref)
        vprev_ref[...] = jnp.zeros_like(vprev_ref)
        s_ref[...] = jnp.zeros_like(s_ref)

    x = x_ref[...]
    xb = x.astype(BF16)

    def proj(lo, width=LANES):
        if lo == RB:
            return _mm(xb, w_r_ref[...])
        return _mm(xb, w_in_ref[:, lo:lo + width])

    lane = lax.broadcasted_iota(jnp.int32, (1, LANES), 1)
    lo_half = lane < HALF
    n_blk = TILE // BLOCK
    n_chunk = TILE // GLA_CHUNK
    n_pair = GLA_HEADS // 2
    pair_w = 2 * GLA_DV

    r_b = proj(RB).astype(BF16)
    kv_new = proj(KA, 2 * LANES)
    q_a = proj(QA, SWA_WIDTH)
    logit = _mm(r_b, w_up_ref[...]) + bg_ref[...]
    log_a = jax.nn.log_sigmoid(logit) / GLA_TAU
    qk_b = proj(QB, 2 * GLA_HEADS * GLA_DK)
    q_b = qk_b[:, :GLA_HEADS * GLA_DK]
    k_b = qk_b[:, GLA_HEADS * GLA_DK:]

    ri = lax.broadcasted_iota(jnp.int32, (TILE, TILE), 0)
    ci = lax.broadcasted_iota(jnp.int32, (TILE, TILE), 1)
    tri = ((ri // GLA_CHUNK == ci // GLA_CHUNK) & (ci <= ri)).astype(BF16)
    la_hi = log_a.astype(BF16)
    la_lo = (log_a - la_hi.astype(F32)).astype(BF16)
    b = _mm(tri, la_hi) + _mm(tri, la_lo)
    v_b = proj(VB, GLA_WIDTH).astype(BF16)

    half = ROT_DIM // 2
    pos = pos_ref[...].astype(F32)
    ones_r = jnp.ones((HALF - ROT_DIM, BLOCK), F32)
    zeros_r = jnp.zeros((HALF - ROT_DIM, BLOCK), F32)
    zeros_h = jnp.zeros((half, BLOCK), F32)
    cos_blk, up_blk, dn_blk = [], [], []
    for i in range(TILE // BLOCK):
        ang = invf_ref[...] * pos[:, i * BLOCK:(i + 1) * BLOCK]
        c8, s8 = jnp.cos(ang), jnp.sin(ang)
        cos_blk.append(jnp.concatenate([c8, c8, ones_r] * 2, axis=0).T)
        up_blk.append(jnp.concatenate([zeros_h, s8, zeros_r] * 2, axis=0).T)
        dn_blk.append(jnp.concatenate([-s8, zeros_h, zeros_r] * 2, axis=0).T)
    cos = jnp.concatenate(cos_blk, axis=0)
    sin_up = jnp.concatenate(up_blk, axis=0)
    sin_dn = jnp.concatenate(dn_blk, axis=0)

    def rope(t):
        return (t * cos + pltpu.roll(t, ROT_DIM // 2, 1) * sin_up
                + pltpu.roll(t, LANES - ROT_DIM // 2, 1) * sin_dn)

    k_new = rope(kv_new[:, :LANES])
    v_new = kv_new[:, LANES:]
    k_cat = jnp.concatenate([kprev_ref[...], k_new], axis=0)
    v_cat = jnp.concatenate([vprev_ref[...], v_new], axis=0)
    kprev_ref[...] = k_new[TILE - BLOCK:]
    vprev_ref[...] = v_new[TILE - BLOCK:]

    k_sw = pltpu.roll(k_cat, HALF, 1)
    k_cb, k_sb = k_cat.astype(BF16), k_sw.astype(BF16)
    k_zero = jnp.zeros_like(k_cb)
    k_even = [jnp.where(lo_half, k_cb, k_zero), jnp.where(lo_half, k_sb, k_zero)]
    k_odd = [jnp.where(lo_half, k_zero, k_sb), jnp.where(lo_half, k_zero, k_cb)]
    v_t = v_cat.T.astype(BF16)

    scale = SWA_HEAD_DIM ** -0.5
    q_r = [(rope(q_a[:, cg * LANES:(cg + 1) * LANES]) * scale).astype(BF16)
           for cg in range(SWA_WIDTH // LANES)]

    swa_tiles = [(h, i) for h in range(SWA_KV_HEADS) for i in range(n_blk)]
    scores = {}
    for h, i in swa_tiles:
        rows = slice(i * BLOCK, (i + 1) * BLOCK)
        kv_rows = slice(i * BLOCK, (i + 2) * BLOCK)
        q_stack = jnp.concatenate([q_r[2 * h][rows], q_r[2 * h + 1][rows]], axis=0)
        scores[h, i] = [_nt(k_even[h][kv_rows], q_stack),
                        _nt(k_odd[h][kv_rows], q_stack)]

    q_dec = (q_b * (GLA_DK ** -0.5) * jnp.exp(b)).astype(BF16)
    k_inv = (k_b * jnp.exp(-b)).astype(BF16)
    causal2 = (lax.broadcasted_iota(jnp.int32, (GLA_CHUNK, LANES), 1) & (HALF - 1)
               ) <= lax.broadcasted_iota(jnp.int32, (GLA_CHUNK, LANES), 0)
    eye = (lax.broadcasted_iota(jnp.int32, (LANES, LANES), 0)
           == lax.broadcasted_iota(jnp.int32, (LANES, LANES), 1))
    gla_tiles = [(p, c) for p in range(n_pair) for c in range(n_chunk)]
    attn, upd, decay_col = {}, {}, {}
    for p, c in gla_tiles:
        cols = slice(p * LANES, (p + 1) * LANES)
        rows = slice(c * GLA_CHUNK, (c + 1) * GLA_CHUNK)
        last = (c + 1) * GLA_CHUNK - 1
        b_last = b[last:last + 1, cols]
        ki2 = k_inv[rows, cols]
        k_end = (k_b[rows, cols] * jnp.exp(b_last - b[rows, cols])).astype(BF16)
        zero = jnp.zeros_like(ki2)
        k_stack = jnp.concatenate([jnp.where(lo_half, ki2, zero),
                                   jnp.where(lo_half, zero, ki2)], axis=0)
        attn[p, c] = jnp.where(causal2, _nt(q_dec[rows, cols], k_stack), 0.0).astype(BF16)
        u = _tn(k_end, v_b[rows, p * pair_w:(p + 1) * pair_w])
        upd[p, c] = jnp.concatenate([u[:GLA_DK, :GLA_DV], u[GLA_DK:, GLA_DV:]], axis=0)
        decay_row = jnp.broadcast_to(jnp.exp(b_last), (LANES, LANES))
        decay_col[p, c] = jnp.sum(jnp.where(eye, decay_row, 0.0), axis=1, keepdims=True)

    gate_a = jax.nn.silu(proj(GA, SWA_WIDTH))
    gate_b = jax.nn.silu(proj(GB, GLA_WIDTH))

    kj = lax.broadcasted_iota(jnp.int32, (2 * BLOCK, BLOCK), 0)
    qj = lax.broadcasted_iota(jnp.int32, (2 * BLOCK, BLOCK), 1)
    in_window = (kj > qj) & (kj <= qj + WINDOW)
    first_blk_mask = in_window & ((kj >= BLOCK) | (t_idx > 0))
    probs, inv_den = {}, {}
    for h, i in swa_tiles:
        mask = first_blk_mask if i == 0 else in_window
        p_t, inv = [], []
        for g in range(4):
            sink = sinks_ref[4 * h + g]
            s = scores[h, i][g % 2][:, (g // 2) * BLOCK:(g // 2 + 1) * BLOCK]
            s = jnp.where(mask, s, -jnp.inf)
            m = jnp.maximum(jnp.max(s, axis=0, keepdims=True), sink)
            e = jnp.exp(s - m)
            denom = jnp.sum(e, axis=0, keepdims=True) + jnp.exp(sink - m)
            inv.append(1.0 / denom)
            p_t.append(e.astype(BF16))
        probs[h, i] = jnp.concatenate(p_t, axis=1)
        inv_den[h, i] = inv

    states = {}
    for p in range(n_pair):
        state = s_ref[p * LANES:(p + 1) * LANES, :]
        for c in range(n_chunk):
            states[p, c] = state.astype(BF16)
            state = state * decay_col[p, c] + upd[p, c]
        s_ref[p * LANES:(p + 1) * LANES, :] = state

    o_swa = {}
    for h, i in swa_tiles:
        kv_rows = slice(i * BLOCK, (i + 2) * BLOCK)
        o_swa[h, i] = _mm(v_t[h * HALF:(h + 1) * HALF, kv_rows], probs[h, i])
    zero_v = jnp.zeros((GLA_CHUNK, GLA_DV), BF16)
    o_gla = {}
    for p, c in gla_tiles:
        rows = slice(c * GLA_CHUNK, (c + 1) * GLA_CHUNK)
        v2 = v_b[rows, p * pair_w:(p + 1) * pair_w]
        st = states[p, c]
        rhs = jnp.concatenate([
            jnp.concatenate([v2[:, :GLA_DV], zero_v], axis=1),
            jnp.concatenate([zero_v, v2[:, GLA_DV:]], axis=1),
            jnp.concatenate([st[:GLA_DK], zero_v], axis=1),
            jnp.concatenate([zero_v, st[GLA_DK:]], axis=1)], axis=0)
        lhs = jnp.concatenate([attn[p, c], q_dec[rows, p * LANES:(p + 1) * LANES]], axis=1)
        o_gla[p, c] = _mm(lhs, rhs)

    for h, i in swa_tiles:
        rows = slice(i * BLOCK, (i + 1) * BLOCK)
        o_t, inv = o_swa[h, i], inv_den[h, i]
        for c in range(2):
            cg = 2 * h + c
            o2_t = jnp.concatenate(
                [o_t[:, (2 * c + par) * BLOCK:(2 * c + par + 1) * BLOCK] * inv[2 * c + par]
                 for par in range(2)], axis=0)
            cg_cols = slice(cg * LANES, (cg + 1) * LANES)
            out = o2_t.T * gate_a[rows, cg_cols]
            mix_ref[rows, cg_cols] = out.astype(BF16)
    norm_w = nw_ref[...]
    for p, c in gla_tiles:
        rows = slice(c * GLA_CHUNK, (c + 1) * GLA_CHUNK)
        for par in range(2):
            o = o_gla[p, c][:, par * GLA_DV:(par + 1) * GLA_DV]
            o = o * lax.rsqrt(jnp.mean(jnp.square(o), axis=-1, keepdims=True) + EPS) * norm_w
            c0 = (2 * p + par) * GLA_DV
            out = o * gate_b[rows, c0:c0 + GLA_DV]
            mix_ref[rows, SWA_WIDTH + c0:SWA_WIDTH + c0 + GLA_DV] = out.astype(BF16)

    y = _mm(mix_ref[...], w_out_ref[...]) + ALPHA * x
    mu = jnp.mean(y, axis=-1, keepdims=True)
    yc = y - mu
    var = jnp.mean(jnp.square(yc), axis=-1, keepdims=True)
    o_ref[...] = yc * lax.rsqrt(var + EPS) * lng_ref[...] + lnb_ref[...]


def _rope_freq_rows():
    half = ROT_DIM // 2
    inv_freq = ROPE_THETA ** (-jnp.arange(half, dtype=F32) / half)
    return jnp.broadcast_to(inv_freq[:, None], (half, BLOCK))


@jax.jit
def kernel(x, positions, w_in, gla_w_gate_up, gla_b_gate, attn_sinks, gla_norm_w, w_out, ln_g, ln_b):
    B, S, D = x.shape
    assert D == D_MODEL and S % TILE == 0 and w_in.shape == (DEPTH, D_MODEL, D_IN_PROJ)
    n_tiles = S // TILE
    pos_rows = positions.reshape(B * n_tiles, 1, TILE)

    def const(shape):
        return pl.BlockSpec((None,) + shape, lambda b, t: (0,) * (len(shape) + 1),
                            pipeline_mode=pl.Buffered(1))

    grid_spec = pl.GridSpec(
        grid=(B, n_tiles),
        in_specs=[
            pl.BlockSpec((None, TILE, D_MODEL), lambda b, t: (b, t, 0)),
            pl.BlockSpec((None, 1, TILE), lambda b, t: (b * n_tiles + t, 0, 0)),
            const((D_MODEL, D_IN_PROJ)),
            const((GLA_RANK, GLA_HEADS * GLA_DK)),
            const((1, GLA_HEADS * GLA_DK)),
            const((1, GLA_DV)),
            const((D_MODEL, D_MODEL)),
            const((1, D_MODEL)),
            const((1, D_MODEL)),
            pl.BlockSpec((ROT_DIM // 2, BLOCK), lambda b, t: (0, 0)),
            pl.BlockSpec(memory_space=pltpu.SMEM),
        ],
        out_specs=pl.BlockSpec((None, TILE, D_MODEL), lambda b, t: (b, t, 0)),
        scratch_shapes=[
            pltpu.VMEM((BLOCK, LANES), F32),
            pltpu.VMEM((BLOCK, LANES), F32),
            pltpu.VMEM((GLA_HEADS * GLA_DK, GLA_DV), F32),
            pltpu.VMEM((TILE, D_MODEL), BF16),
            pltpu.VMEM((D_MODEL, RB), BF16),
            pltpu.VMEM((D_MODEL, LANES), BF16),
            pltpu.VMEM((LANES, GLA_HEADS * GLA_DK), BF16),
            pltpu.VMEM((D_MODEL, D_MODEL), BF16),
        ],
    )
    row = lambda a: a.reshape(DEPTH, 1, -1)
    return pl.pallas_call(
        _layer_kernel,
        grid_spec=grid_spec,
        out_shape=jax.ShapeDtypeStruct((B, S, D_MODEL), x.dtype),
        compiler_params=pltpu.CompilerParams(
            dimension_semantics=("arbitrary", "arbitrary"),
            vmem_limit_bytes=VMEM_LIMIT_BYTES),
        name="hymba_layer",
    )(x, pos_rows, w_in, gla_w_gate_up, row(gla_b_gate), row(gla_norm_w),
      w_out, row(ln_g), row(ln_b), _rope_freq_rows(), attn_sinks[0])
```

```python
import numpy as np
import jax
import jax.numpy as jnp
from jax import lax
from jax.experimental import pallas as pl
from jax.experimental.pallas import tpu as pltpu

D_MODEL = 1024
SWA_WIDTH = 512
SWA_HEAD_DIM = 64
SWA_Q_HEADS = 8
SWA_KV_HEADS = 2
WINDOW = 128
BLOCK = 128
ROPE_THETA = 500000.0
ROT_DIM = 16
GLA_WIDTH = 512
GLA_HEADS = 4
GLA_DK = 64
GLA_DV = 128
GLA_RANK = 16
GLA_TAU = 16.0
GLA_CHUNK = 64
EPS = 1e-5
DEPTH = 1
ALPHA = (2 * DEPTH) ** 0.25

LANES = 128
HALF = LANES // 2

QA, KA, VA, GA, QB, KB, VB, GB, RB = 0, 512, 640, 768, 1280, 1536, 1792, 2304, 2816
D_IN_PROJ = RB + GLA_RANK

TILE = 256
W_CAST_ROWS = 128
VMEM_LIMIT_BYTES = 56 * 1024 * 1024

F32 = jnp.float32
BF16 = jnp.bfloat16


def _nt(a, b):
    return lax.dot_general(a, b, (((1,), (1,)), ((), ())), preferred_element_type=F32)


def _tn(a, b):
    return lax.dot_general(a, b, (((0,), (0,)), ((), ())), preferred_element_type=F32)


def _mm(a, b):
    return jnp.dot(a, b, preferred_element_type=F32)


def _layer_kernel(x_ref, pos_ref, w_in_t_f32, w_up_f32, bg_ref, nw_ref, w_out_f32,
                  lng_ref, lnb_ref, invf_ref, sinks_ref,
                  o_ref, kprev_ref, vprev_ref, s_ref, mix_ref,
                  w_in_ref, w_r_ref, w_up_ref, w_out_ref):
    t_idx = pl.program_id(1)

    @pl.when((pl.program_id(0) == 0) & (t_idx == 0))
    def _cast_weights():
        def cast_group(g, carry):
            rows = pl.ds(pl.multiple_of(g * LANES, LANES), LANES)
            w_in_ref[g] = w_in_t_f32[rows, :].T.astype(BF16)
            return carry
        lax.fori_loop(0, RB // LANES, cast_group, 0)
        rank_rows = jnp.concatenate(
            [w_in_t_f32[RB:, :], jnp.zeros((LANES - GLA_RANK, D_MODEL), F32)], axis=0)
        w_r_ref[...] = rank_rows.T.astype(BF16)

        def cast_rows(r, carry):
            rows = pl.ds(pl.multiple_of(r * W_CAST_ROWS, W_CAST_ROWS), W_CAST_ROWS)
            w_out_ref[rows, :] = w_out_f32[rows, :].astype(BF16)
            return carry
        lax.fori_loop(0, D_MODEL // W_CAST_ROWS, cast_rows, 0)
        w_up_ref[...] = jnp.concatenate(
            [w_up_f32[...], jnp.zeros((LANES - GLA_RANK, GLA_HEADS * GLA_DK), F32)],
            axis=0).astype(BF16)

    @pl.when(t_idx == 0)
    def _reset_carries():
        kprev_ref[...] = jnp.zeros_like(kprev_ref)
        vprev_ref[...] = jnp.zeros_like(vprev_ref)
        s_ref[...] = jnp.zeros_like(s_ref)

    x = x_ref[...]
    xb = x.astype(BF16)

    def proj(lo, width=LANES):
        if lo == RB:
            return _mm(xb, w_r_ref[...])
        groups = [w_in_ref[g] for g in range(lo // LANES, (lo + width) // LANES)]
        return _mm(xb, jnp.concatenate(groups, axis=1))

    lane = lax.broadcasted_iota(jnp.int32, (1, LANES), 1)
    lo_half = lane < HALF
    n_blk = TILE // BLOCK
    n_chunk = TILE // GLA_CHUNK
    n_pair = GLA_HEADS // 2
    pair_w = 2 * GLA_DV

    r_b = proj(RB).astype(BF16)
    kv_new = proj(KA, 2 * LANES)
    q_a = proj(QA, SWA_WIDTH)
    logit = _mm(r_b, w_up_ref[...]) + bg_ref[...]
    log_a = jax.nn.log_sigmoid(logit) / GLA_TAU
    qk_b = proj(QB, 2 * GLA_HEADS * GLA_DK)
    q_b = qk_b[:, :GLA_HEADS * GLA_DK]
    k_b = qk_b[:, GLA_HEADS * GLA_DK:]

    ri = lax.broadcasted_iota(jnp.int32, (TILE, TILE), 0)
    ci = lax.broadcasted_iota(jnp.int32, (TILE, TILE), 1)
    tri = ((ri // GLA_CHUNK == ci // GLA_CHUNK) & (ci <= ri)).astype(BF16)
    la_hi = log_a.astype(BF16)
    la_lo = (log_a - la_hi.astype(F32)).astype(BF16)
    b = _mm(tri, la_hi) + _mm(tri, la_lo)
    v_b = proj(VB, GLA_WIDTH).astype(BF16)

    half = ROT_DIM // 2
    pos = pos_ref[...].astype(F32)
    ones_r = jnp.ones((HALF - ROT_DIM, BLOCK), F32)
    zeros_r = jnp.zeros((HALF - ROT_DIM, BLOCK), F32)
    zeros_h = jnp.zeros((half, BLOCK), F32)
    cos_blk, up_blk, dn_blk = [], [], []
    for i in range(TILE // BLOCK):
        ang = invf_ref[...] * pos[:, i * BLOCK:(i + 1) * BLOCK]
        c8, s8 = jnp.cos(ang), jnp.sin(ang)
        cos_blk.append(jnp.concatenate([c8, c8, ones_r] * 2, axis=0).T)
        up_blk.append(jnp.concatenate([zeros_h, s8, zeros_r] * 2, axis=0).T)
        dn_blk.append(jnp.concatenate([-s8, zeros_h, zeros_r] * 2, axis=0).T)
    cos = jnp.concatenate(cos_blk, axis=0)
    sin_up = jnp.concatenate(up_blk, axis=0)
    sin_dn = jnp.concatenate(dn_blk, axis=0)

    def rope(t):
        return (t * cos + pltpu.roll(t, ROT_DIM // 2, 1) * sin_up
                + pltpu.roll(t, LANES - ROT_DIM // 2, 1) * sin_dn)

    k_new = rope(kv_new[:, :LANES])
    v_new = kv_new[:, LANES:]
    k_cat = jnp.concatenate([kprev_ref[...], k_new], axis=0)
    v_cat = jnp.concatenate([vprev_ref[...], v_new], axis=0)
    kprev_ref[...] = k_new[TILE - BLOCK:]
    vprev_ref[...] = v_new[TILE - BLOCK:]

    k_sw = pltpu.roll(k_cat, HALF, 1)
    k_cb, k_sb = k_cat.astype(BF16), k_sw.astype(BF16)
    k_zero = jnp.zeros_like(k_cb)
    k_even = [jnp.where(lo_half, k_cb, k_zero), jnp.where(lo_half, k_sb, k_zero)]
    k_odd = [jnp.where(lo_half, k_zero, k_sb), jnp.where(lo_half, k_zero, k_cb)]
    v_t = v_cat.T.astype(BF16)

    scale = SWA_HEAD_DIM ** -0.5
    q_r = [(rope(q_a[:, cg * LANES:(cg + 1) * LANES]) * scale).astype(BF16)
           for cg in range(SWA_WIDTH // LANES)]

    swa_tiles = [(h, i) for h in range(SWA_KV_HEADS) for i in range(n_blk)]
    scores = {}
    for h, i in swa_tiles:
        rows = slice(i * BLOCK, (i + 1) * BLOCK)
        kv_rows = slice(i * BLOCK, (i + 2) * BLOCK)
        q_stack = jnp.concatenate([q_r[2 * h][rows], q_r[2 * h + 1][rows]], axis=0)
        scores[h, i] = [_nt(k_even[h][kv_rows], q_stack),
                        _nt(k_odd[h][kv_rows], q_stack)]

    q_dec = (q_b * (GLA_DK ** -0.5) * jnp.exp(b)).astype(BF16)
    k_inv = (k_b * jnp.exp(-b)).astype(BF16)
    causal2 = (lax.broadcasted_iota(jnp.int32, (GLA_CHUNK, LANES), 1) & (HALF - 1)
               ) <= lax.broadcasted_iota(jnp.int32, (GLA_CHUNK, LANES), 0)
    eye = (lax.broadcasted_iota(jnp.int32, (LANES, LANES), 0)
           == lax.broadcasted_iota(jnp.int32, (LANES, LANES), 1))
    gla_tiles = [(p, c) for p in range(n_pair) for c in range(n_chunk)]
    attn, upd, decay_col = {}, {}, {}
    for p, c in gla_tiles:
        cols = slice(p * LANES, (p + 1) * LANES)
        rows = slice(c * GLA_CHUNK, (c + 1) * GLA_CHUNK)
        last = (c + 1) * GLA_CHUNK - 1
        b_last = b[last:last + 1, cols]
        ki2 = k_inv[rows, cols]
        k_end = (k_b[rows, cols] * jnp.exp(b_last - b[rows, cols])).astype(BF16)
        zero = jnp.zeros_like(ki2)
        k_stack = jnp.concatenate([jnp.where(lo_half, ki2, zero),
                                   jnp.where(lo_half, zero, ki2)], axis=0)
        attn[p, c] = jnp.where(causal2, _nt(q_dec[rows, cols], k_stack), 0.0).astype(BF16)
        u = _tn(k_end, v_b[rows, p * pair_w:(p + 1) * pair_w])
        upd[p, c] = jnp.concatenate([u[:GLA_DK, :GLA_DV], u[GLA_DK:, GLA_DV:]], axis=0)
        decay_row = jnp.broadcast_to(jnp.exp(b_last), (LANES, LANES))
        decay_col[p, c] = jnp.sum(jnp.where(eye, decay_row, 0.0), axis=1, keepdims=True)

    gate_a = jax.nn.silu(proj(GA, SWA_WIDTH))
    gate_b = jax.nn.silu(proj(GB, GLA_WIDTH))

    kj = lax.broadcasted_iota(jnp.int32, (2 * BLOCK, BLOCK), 0)
    qj = lax.broadcasted_iota(jnp.int32, (2 * BLOCK, BLOCK), 1)
    in_window = (kj > qj) & (kj <= qj + WINDOW)
    first_blk_mask = in_window & ((kj >= BLOCK) | (t_idx > 0))
    probs, inv_den = {}, {}
    for h, i in swa_tiles:
        mask = first_blk_mask if i == 0 else in_window
        p_t, inv = [], []
        for g in range(4):
            sink = sinks_ref[4 * h + g]
            s = scores[h, i][g % 2][:, (g // 2) * BLOCK:(g // 2 + 1) * BLOCK]
            s = jnp.where(mask, s, -jnp.inf)
            m = jnp.maximum(jnp.max(s, axis=0, keepdims=True), sink)
            e = jnp.exp(s - m)
            denom = jnp.sum(e, axis=0, keepdims=True) + jnp.exp(sink - m)
            inv.append(1.0 / denom)
            p_t.append(e.astype(BF16))
        probs[h, i] = jnp.concatenate(p_t, axis=1)
        inv_den[h, i] = inv

    states = {}
    for p in range(n_pair):
        state = s_ref[p * LANES:(p + 1) * LANES, :]
        for c in range(n_chunk):
            states[p, c] = state.astype(BF16)
            state = state * decay_col[p, c] + upd[p, c]
        s_ref[p * LANES:(p + 1) * LANES, :] = state

    o_swa = {}
    for h, i in swa_tiles:
        kv_rows = slice(i * BLOCK, (i + 2) * BLOCK)
        o_swa[h, i] = _mm(v_t[h * HALF:(h + 1) * HALF, kv_rows], probs[h, i])
    zero_v = jnp.zeros((GLA_CHUNK, GLA_DV), BF16)
    o_gla = {}
    for p, c in gla_tiles:
        rows = slice(c * GLA_CHUNK, (c + 1) * GLA_CHUNK)
        v2 = v_b[rows, p * pair_w:(p + 1) * pair_w]
        st = states[p, c]
        rhs = jnp.concatenate([
            jnp.concatenate([v2[:, :GLA_DV], zero_v], axis=1),
            jnp.concatenate([zero_v, v2[:, GLA_DV:]], axis=1),
            jnp.concatenate([st[:GLA_DK], zero_v], axis=1),
            jnp.concatenate([zero_v, st[GLA_DK:]], axis=1)], axis=0)
        lhs = jnp.concatenate([attn[p, c], q_dec[rows, p * LANES:(p + 1) * LANES]], axis=1)
        o_gla[p, c] = _mm(lhs, rhs)

    for h, i in swa_tiles:
        rows = slice(i * BLOCK, (i + 1) * BLOCK)
        o_t, inv = o_swa[h, i], inv_den[h, i]
        for c in range(2):
            cg = 2 * h + c
            o2_t = jnp.concatenate(
                [o_t[:, (2 * c + par) * BLOCK:(2 * c + par + 1) * BLOCK] * inv[2 * c + par]
                 for par in range(2)], axis=0)
            cg_cols = slice(cg * LANES, (cg + 1) * LANES)
            out = o2_t.T * gate_a[rows, cg_cols]
            mix_ref[rows, cg_cols] = out.astype(BF16)
    norm_w = nw_ref[...]
    for p, c in gla_tiles:
        rows = slice(c * GLA_CHUNK, (c + 1) * GLA_CHUNK)
        for par in range(2):
            o = o_gla[p, c][:, par * GLA_DV:(par + 1) * GLA_DV]
            o = o * lax.rsqrt(jnp.mean(jnp.square(o), axis=-1, keepdims=True) + EPS) * norm_w
            c0 = (2 * p + par) * GLA_DV
            out = o * gate_b[rows, c0:c0 + GLA_DV]
            mix_ref[rows, SWA_WIDTH + c0:SWA_WIDTH + c0 + GLA_DV] = out.astype(BF16)

    y = _mm(mix_ref[...], w_out_ref[...]) + ALPHA * x
    mu = jnp.mean(y, axis=-1, keepdims=True)
    yc = y - mu
    var = jnp.mean(jnp.square(yc), axis=-1, keepdims=True)
    o_ref[...] = yc * lax.rsqrt(var + EPS) * lng_ref[...] + lnb_ref[...]


def _rope_freq_rows():
    half = ROT_DIM // 2
    inv_freq = ROPE_THETA ** (-jnp.arange(half, dtype=F32) / half)
    return jnp.broadcast_to(inv_freq[:, None], (half, BLOCK))


@jax.jit
def kernel(x, positions, w_in, gla_w_gate_up, gla_b_gate, attn_sinks, gla_norm_w, w_out, ln_g, ln_b):
    B, S, D = x.shape
    assert D == D_MODEL and S % TILE == 0 and w_in.shape == (DEPTH, D_MODEL, D_IN_PROJ)
    n_tiles = S // TILE
    pos_rows = positions.reshape(B * n_tiles, 1, TILE)

    def const(shape):
        return pl.BlockSpec((None,) + shape, lambda b, t: (0,) * (len(shape) + 1),
                            pipeline_mode=pl.Buffered(1))

    grid_spec = pl.GridSpec(
        grid=(B, n_tiles),
        in_specs=[
            pl.BlockSpec((None, TILE, D_MODEL), lambda b, t: (b, t, 0)),
            pl.BlockSpec((None, 1, TILE), lambda b, t: (b * n_tiles + t, 0, 0)),
            const((D_IN_PROJ, D_MODEL)),
            const((GLA_RANK, GLA_HEADS * GLA_DK)),
            const((1, GLA_HEADS * GLA_DK)),
            const((1, GLA_DV)),
            const((D_MODEL, D_MODEL)),
            const((1, D_MODEL)),
            const((1, D_MODEL)),
            pl.BlockSpec((ROT_DIM // 2, BLOCK), lambda b, t: (0, 0)),
            pl.BlockSpec(memory_space=pltpu.SMEM),
        ],
        out_specs=pl.BlockSpec((None, TILE, D_MODEL), lambda b, t: (b, t, 0)),
        scratch_shapes=[
            pltpu.VMEM((BLOCK, LANES), F32),
            pltpu.VMEM((BLOCK, LANES), F32),
            pltpu.VMEM((GLA_HEADS * GLA_DK, GLA_DV), F32),
            pltpu.VMEM((TILE, D_MODEL), BF16),
            pltpu.VMEM((RB // LANES, D_MODEL, LANES), BF16),
            pltpu.VMEM((D_MODEL, LANES), BF16),
            pltpu.VMEM((LANES, GLA_HEADS * GLA_DK), BF16),
            pltpu.VMEM((D_MODEL, D_MODEL), BF16),
        ],
    )
    row = lambda a: a.reshape(DEPTH, 1, -1)
    return pl.pallas_call(
        _layer_kernel,
        grid_spec=grid_spec,
        out_shape=jax.ShapeDtypeStruct((B, S, D_MODEL), x.dtype),
        compiler_params=pltpu.CompilerParams(
            dimension_semantics=("arbitrary", "arbitrary"),
            vmem_limit_bytes=VMEM_LIMIT_BYTES),
        name="hymba_layer",
    )(x, pos_rows, jnp.swapaxes(w_in, 1, 2), gla_w_gate_up, row(gla_b_gate), row(gla_norm_w),
      w_out, row(ln_g), row(ln_b), _rope_freq_rows(), attn_sinks[0])
```

```python
import jax
import jax.numpy as jnp
from jax import lax
from jax.experimental import pallas as pl
from jax.experimental.pallas import tpu as pltpu

D_MODEL = 1024
SWA_WIDTH = 512
SWA_HEAD_DIM = 64
SWA_Q_HEADS = 8
SWA_KV_HEADS = 2
WINDOW = 128
BLOCK = 128
ROPE_THETA = 500000.0
ROT_DIM = 16
GLA_WIDTH = 512
GLA_HEADS = 4
GLA_DK = 64
GLA_DV = 128
GLA_RANK = 16
GLA_TAU = 16.0
GLA_CHUNK = 64
EPS = 1e-5
DEPTH = 1
ALPHA = (2 * DEPTH) ** 0.25

LANES = 128
HALF = LANES // 2

QA, KA, VA, GA, QB, KB, VB, GB, RB = 0, 512, 640, 768, 1280, 1536, 1792, 2304, 2816
D_IN_PROJ = RB + GLA_RANK

TILE = 512
CUM_ROWS = 256
W_CAST_ROWS = 128
VMEM_LIMIT_BYTES = 60 * 1024 * 1024

F32 = jnp.float32
BF16 = jnp.bfloat16


def _nt(a, b):
    return lax.dot_general(a, b, (((1,), (1,)), ((), ())), preferred_element_type=F32)


def _tn(a, b):
    return lax.dot_general(a, b, (((0,), (0,)), ((), ())), preferred_element_type=F32)


def _mm(a, b):
    return jnp.dot(a, b, preferred_element_type=F32)


def _layer_kernel(x_ref, pos_ref, w_in_t_f32, w_up_f32, bg_ref, nw_ref, w_out_f32,
                  lng_ref, lnb_ref, invf_ref, sinks_ref,
                  o_ref, kprev_ref, vprev_ref, s_ref, mix_ref,
                  w_in_ref, w_r_ref, w_up_ref, w_out_ref):
    t_idx = pl.program_id(1)

    @pl.when((pl.program_id(0) == 0) & (t_idx == 0))
    def _cast_weights():
        def cast_group(g, carry):
            rows = pl.ds(pl.multiple_of(g * LANES, LANES), LANES)
            w_in_ref[g] = w_in_t_f32[rows, :].T.astype(BF16)
            return carry
        lax.fori_loop(0, RB // LANES, cast_group, 0)
        rank_rows = jnp.concatenate(
            [w_in_t_f32[RB:, :], jnp.zeros((LANES - GLA_RANK, D_MODEL), F32)], axis=0)
        w_r_ref[...] = rank_rows.T.astype(BF16)

        def cast_rows(r, carry):
            rows = pl.ds(pl.multiple_of(r * W_CAST_ROWS, W_CAST_ROWS), W_CAST_ROWS)
            w_out_ref[rows, :] = w_out_f32[rows, :].astype(BF16)
            return carry
        lax.fori_loop(0, D_MODEL // W_CAST_ROWS, cast_rows, 0)
        w_up_ref[...] = jnp.concatenate(
            [w_up_f32[...], jnp.zeros((LANES - GLA_RANK, GLA_HEADS * GLA_DK), F32)],
            axis=0).astype(BF16)

    @pl.when(t_idx == 0)
    def _reset_carries():
        kprev_ref[...] = jnp.zeros_like(kprev_ref)
        vprev_ref[...] = jnp.zeros_like(vprev_ref)
        s_ref[...] = jnp.zeros_like(s_ref)

    x = x_ref[...]
    xb = x.astype(BF16)

    def proj(lo, width=LANES):
        if lo == RB:
            return _mm(xb, w_r_ref[...])
        groups = [w_in_ref[g] for g in range(lo // LANES, (lo + width) // LANES)]
        return _mm(xb, jnp.concatenate(groups, axis=1))

    lane = lax.broadcasted_iota(jnp.int32, (1, LANES), 1)
    lo_half = lane < HALF
    n_blk = TILE // BLOCK
    n_chunk = TILE // GLA_CHUNK
    n_pair = GLA_HEADS // 2
    pair_w = 2 * GLA_DV

    r_b = proj(RB).astype(BF16)
    kv_new = proj(KA, 2 * LANES)
    q_a = proj(QA, SWA_WIDTH)
    logit = _mm(r_b, w_up_ref[...]) + bg_ref[...]
    log_a = jax.nn.log_sigmoid(logit) / GLA_TAU
    qk_b = proj(QB, 2 * GLA_HEADS * GLA_DK)
    q_b = qk_b[:, :GLA_HEADS * GLA_DK]
    k_b = qk_b[:, GLA_HEADS * GLA_DK:]

    ri = lax.broadcasted_iota(jnp.int32, (CUM_ROWS, CUM_ROWS), 0)
    ci = lax.broadcasted_iota(jnp.int32, (CUM_ROWS, CUM_ROWS), 1)
    tri = ((ri // GLA_CHUNK == ci // GLA_CHUNK) & (ci <= ri)).astype(BF16)
    la_hi = log_a.astype(BF16)
    la_lo = (log_a - la_hi.astype(F32)).astype(BF16)
    b = jnp.concatenate(
        [_mm(tri, la_hi[r:r + CUM_ROWS]) + _mm(tri, la_lo[r:r + CUM_ROWS])
         for r in range(0, TILE, CUM_ROWS)], axis=0)
    v_b = proj(VB, GLA_WIDTH).astype(BF16)

    half = ROT_DIM // 2
    pos = pos_ref[...].astype(F32)
    ones_r = jnp.ones((HALF - ROT_DIM, BLOCK), F32)
    zeros_r = jnp.zeros((HALF - ROT_DIM, BLOCK), F32)
    zeros_h = jnp.zeros((half, BLOCK), F32)
    cos_blk, up_blk, dn_blk = [], [], []
    for i in range(TILE // BLOCK):
        ang = invf_ref[...] * pos[:, i * BLOCK:(i + 1) * BLOCK]
        c8, s8 = jnp.cos(ang), jnp.sin(ang)
        cos_blk.append(jnp.concatenate([c8, c8, ones_r] * 2, axis=0).T)
        up_blk.append(jnp.concatenate([zeros_h, s8, zeros_r] * 2, axis=0).T)
        dn_blk.append(jnp.concatenate([-s8, zeros_h, zeros_r] * 2, axis=0).T)
    cos = jnp.concatenate(cos_blk, axis=0)
    sin_up = jnp.concatenate(up_blk, axis=0)
    sin_dn = jnp.concatenate(dn_blk, axis=0)

    def rope(t):
        return (t * cos + pltpu.roll(t, ROT_DIM // 2, 1) * sin_up
                + pltpu.roll(t, LANES - ROT_DIM // 2, 1) * sin_dn)

    k_new = rope(kv_new[:, :LANES])
    v_new = kv_new[:, LANES:]
    k_cat = jnp.concatenate([kprev_ref[...], k_new], axis=0)
    v_cat = jnp.concatenate([vprev_ref[...], v_new], axis=0)
    kprev_ref[...] = k_new[TILE - BLOCK:]
    vprev_ref[...] = v_new[TILE - BLOCK:]

    k_sw = pltpu.roll(k_cat, HALF, 1)
    k_cb, k_sb = k_cat.astype(BF16), k_sw.astype(BF16)
    k_zero = jnp.zeros_like(k_cb)
    k_even = [jnp.where(lo_half, k_cb, k_zero), jnp.where(lo_half, k_sb, k_zero)]
    k_odd = [jnp.where(lo_half, k_zero, k_sb), jnp.where(lo_half, k_zero, k_cb)]
    v_t = v_cat.T.astype(BF16)

    scale = SWA_HEAD_DIM ** -0.5
    q_r = [(rope(q_a[:, cg * LANES:(cg + 1) * LANES]) * scale).astype(BF16)
           for cg in range(SWA_WIDTH // LANES)]

    swa_tiles = [(h, i) for i in range(n_blk) for h in range(SWA_KV_HEADS)]
    scores = {}
    for h, i in swa_tiles:
        rows = slice(i * BLOCK, (i + 1) * BLOCK)
        kv_rows = slice(i * BLOCK, (i + 2) * BLOCK)
        q_stack = jnp.concatenate([q_r[2 * h][rows], q_r[2 * h + 1][rows]], axis=0)
        scores[h, i] = [_nt(k_even[h][kv_rows], q_stack),
                        _nt(k_odd[h][kv_rows], q_stack)]

    q_dec = (q_b * (GLA_DK ** -0.5) * jnp.exp(b)).astype(BF16)
    k_inv = (k_b * jnp.exp(-b)).astype(BF16)
    causal2 = (lax.broadcasted_iota(jnp.int32, (GLA_CHUNK, LANES), 1) & (HALF - 1)
               ) <= lax.broadcasted_iota(jnp.int32, (GLA_CHUNK, LANES), 0)
    eye = (lax.broadcasted_iota(jnp.int32, (LANES, LANES), 0)
           == lax.broadcasted_iota(jnp.int32, (LANES, LANES), 1))
    gla_tiles = [(p, c) for c in range(n_chunk) for p in range(n_pair)]
    attn, upd, decay_col = {}, {}, {}
    for p, c in gla_tiles:
        cols = slice(p * LANES, (p + 1) * LANES)
        rows = slice(c * GLA_CHUNK, (c + 1) * GLA_CHUNK)
        last = (c + 1) * GLA_CHUNK - 1
        b_last = b[last:last + 1, cols]
        ki2 = k_inv[rows, cols]
        k_end = (k_b[rows, cols] * jnp.exp(b_last - b[rows, cols])).astype(BF16)
        zero = jnp.zeros_like(ki2)
        k_stack = jnp.concatenate([jnp.where(lo_half, ki2, zero),
                                   jnp.where(lo_half, zero, ki2)], axis=0)
        attn[p, c] = jnp.where(causal2, _nt(q_dec[rows, cols], k_stack), 0.0).astype(BF16)
        u = _tn(k_end, v_b[rows, p * pair_w:(p + 1) * pair_w])
        upd[p, c] = jnp.concatenate([u[:GLA_DK, :GLA_DV], u[GLA_DK:, GLA_DV:]], axis=0)
        decay_row = jnp.broadcast_to(jnp.exp(b_last), (LANES, LANES))
        decay_col[p, c] = jnp.sum(jnp.where(eye, decay_row, 0.0), axis=1, keepdims=True)

    gate_a = jax.nn.silu(proj(GA, SWA_WIDTH))
    gate_b = jax.nn.silu(proj(GB, GLA_WIDTH))

    kj = lax.broadcasted_iota(jnp.int32, (2 * BLOCK, BLOCK), 0)
    qj = lax.broadcasted_iota(jnp.int32, (2 * BLOCK, BLOCK), 1)
    in_window = (kj > qj) & (kj <= qj + WINDOW)
    first_blk_mask = in_window & ((kj >= BLOCK) | (t_idx > 0))
    probs, inv_den = {}, {}
    for h, i in swa_tiles:
        mask = first_blk_mask if i == 0 else in_window
        p_t, inv = [], []
        for g in range(4):
            sink = sinks_ref[4 * h + g]
            s = scores[h, i][g % 2][:, (g // 2) * BLOCK:(g // 2 + 1) * BLOCK]
            s = jnp.where(mask, s, -jnp.inf)
            m = jnp.maximum(jnp.max(s, axis=0, keepdims=True), sink)
            e = jnp.exp(s - m)
            denom = jnp.sum(e, axis=0, keepdims=True) + jnp.exp(sink - m)
            inv.append(1.0 / denom)
            p_t.append(e.astype(BF16))
        probs[h, i] = jnp.concatenate(p_t, axis=1)
        inv_den[h, i] = inv

    states = {}
    for p in range(n_pair):
        state = s_ref[p * LANES:(p + 1) * LANES, :]
        for c in range(n_chunk):
            states[p, c] = state.astype(BF16)
            state = state * decay_col[p, c] + upd[p, c]
        s_ref[p * LANES:(p + 1) * LANES, :] = state

    o_swa = {}
    for h, i in swa_tiles:
        kv_rows = slice(i * BLOCK, (i + 2) * BLOCK)
        o_swa[h, i] = _mm(v_t[h * HALF:(h + 1) * HALF, kv_rows], probs[h, i])
    zero_v = jnp.zeros((GLA_CHUNK, GLA_DV), BF16)
    o_gla = {}
    for p, c in gla_tiles:
        rows = slice(c * GLA_CHUNK, (c + 1) * GLA_CHUNK)
        v2 = v_b[rows, p * pair_w:(p + 1) * pair_w]
        st = states[p, c]
        rhs = jnp.concatenate([
            jnp.concatenate([v2[:, :GLA_DV], zero_v], axis=1),
            jnp.concatenate([zero_v, v2[:, GLA_DV:]], axis=1),
            jnp.concatenate([st[:GLA_DK], zero_v], axis=1),
            jnp.concatenate([zero_v, st[GLA_DK:]], axis=1)], axis=0)
        lhs = jnp.concatenate([attn[p, c], q_dec[rows, p * LANES:(p + 1) * LANES]], axis=1)
        o_gla[p, c] = _mm(lhs, rhs)

    for h, i in swa_tiles:
        rows = slice(i * BLOCK, (i + 1) * BLOCK)
        o_t, inv = o_swa[h, i], inv_den[h, i]
        for c in range(2):
            cg = 2 * h + c
            o2_t = jnp.concatenate(
                [o_t[:, (2 * c + par) * BLOCK:(2 * c + par + 1) * BLOCK] * inv[2 * c + par]
                 for par in range(2)], axis=0)
            cg_cols = slice(cg * LANES, (cg + 1) * LANES)
            out = o2_t.T * gate_a[rows, cg_cols]
            mix_ref[rows, cg_cols] = out.astype(BF16)
    norm_w = nw_ref[...]
    for p, c in gla_tiles:
        rows = slice(c * GLA_CHUNK, (c + 1) * GLA_CHUNK)
        for par in range(2):
            o = o_gla[p, c][:, par * GLA_DV:(par + 1) * GLA_DV]
            o = o * lax.rsqrt(jnp.mean(jnp.square(o), axis=-1, keepdims=True) + EPS) * norm_w
            c0 = (2 * p + par) * GLA_DV
            out = o * gate_b[rows, c0:c0 + GLA_DV]
            mix_ref[rows, SWA_WIDTH + c0:SWA_WIDTH + c0 + GLA_DV] = out.astype(BF16)

    y = _mm(mix_ref[...], w_out_ref[...]) + ALPHA * x
    mu = jnp.mean(y, axis=-1, keepdims=True)
    yc = y - mu
    var = jnp.mean(jnp.square(yc), axis=-1, keepdims=True)
    o_ref[...] = yc * lax.rsqrt(var + EPS) * lng_ref[...] + lnb_ref[...]


def _rope_freq_rows():
    half = ROT_DIM // 2
    inv_freq = ROPE_THETA ** (-jnp.arange(half, dtype=F32) / half)
    return jnp.broadcast_to(inv_freq[:, None], (half, BLOCK))


@jax.jit
def kernel(x, positions, w_in, gla_w_gate_up, gla_b_gate, attn_sinks, gla_norm_w, w_out, ln_g, ln_b):
    B, S, D = x.shape
    assert D == D_MODEL and S % TILE == 0 and w_in.shape == (DEPTH, D_MODEL, D_IN_PROJ)
    n_tiles = S // TILE
    pos_rows = positions.reshape(B * n_tiles, 1, TILE)

    def const(shape):
        return pl.BlockSpec((None,) + shape, lambda b, t: (0,) * (len(shape) + 1),
                            pipeline_mode=pl.Buffered(1))

    grid_spec = pl.GridSpec(
        grid=(B, n_tiles),
        in_specs=[
            pl.BlockSpec((None, TILE, D_MODEL), lambda b, t: (b, t, 0)),
            pl.BlockSpec((None, 1, TILE), lambda b, t: (b * n_tiles + t, 0, 0)),
            const((D_IN_PROJ, D_MODEL)),
            const((GLA_RANK, GLA_HEADS * GLA_DK)),
            const((1, GLA_HEADS * GLA_DK)),
            const((1, GLA_DV)),
            const((D_MODEL, D_MODEL)),
            const((1, D_MODEL)),
            const((1, D_MODEL)),
            pl.BlockSpec((ROT_DIM // 2, BLOCK), lambda b, t: (0, 0)),
            pl.BlockSpec(memory_space=pltpu.SMEM),
        ],
        out_specs=pl.BlockSpec((None, TILE, D_MODEL), lambda b, t: (b, t, 0)),
        scratch_shapes=[
            pltpu.VMEM((BLOCK, LANES), F32),
            pltpu.VMEM((BLOCK, LANES), F32),
            pltpu.VMEM((GLA_HEADS * GLA_DK, GLA_DV), F32),
            pltpu.VMEM((TILE, D_MODEL), BF16),
            pltpu.VMEM((RB // LANES, D_MODEL, LANES), BF16),
            pltpu.VMEM((D_MODEL, LANES), BF16),
            pltpu.VMEM((LANES, GLA_HEADS * GLA_DK), BF16),
            pltpu.VMEM((D_MODEL, D_MODEL), BF16),
        ],
    )
    row = lambda a: a.reshape(DEPTH, 1, -1)
    return pl.pallas_call(
        _layer_kernel,
        grid_spec=grid_spec,
        out_shape=jax.ShapeDtypeStruct((B, S, D_MODEL), x.dtype),
        compiler_params=pltpu.CompilerParams(
            dimension_semantics=("arbitrary", "arbitrary"),
            vmem_limit_bytes=VMEM_LIMIT_BYTES),
        name="hymba_layer",
    )(x, pos_rows, jnp.swapaxes(w_in, 1, 2), gla_w_gate_up, row(gla_b_gate), row(gla_norm_w),
      w_out, row(ln_g), row(ln_b), _rope_freq_rows(), attn_sinks[0])
```

```python
import jax
import jax.numpy as jnp
from jax import lax
from jax.experimental import pallas as pl
from jax.experimental.pallas import tpu as pltpu

D_MODEL = 1024
SWA_WIDTH = 512
SWA_HEAD_DIM = 64
SWA_Q_HEADS = 8
SWA_KV_HEADS = 2
WINDOW = 128
BLOCK = 128
ROPE_THETA = 500000.0
ROT_DIM = 16
GLA_WIDTH = 512
GLA_HEADS = 4
GLA_DK = 64
GLA_DV = 128
GLA_RANK = 16
GLA_TAU = 16.0
GLA_CHUNK = 64
EPS = 1e-5
DEPTH = 1
ALPHA = (2 * DEPTH) ** 0.25

LANES = 128
HALF = LANES // 2

QA, KA, VA, GA, QB, KB, VB, GB, RB = 0, 512, 640, 768, 1280, 1536, 1792, 2304, 2816
D_IN_PROJ = RB + GLA_RANK

TILE = 512
CUM_ROWS = 256
W_CAST_ROWS = 128
VMEM_LIMIT_BYTES = 60 * 1024 * 1024

F32 = jnp.float32
BF16 = jnp.bfloat16


def _nt(a, b):
    return lax.dot_general(a, b, (((1,), (1,)), ((), ())), preferred_element_type=F32)


def _tn(a, b):
    return lax.dot_general(a, b, (((0,), (0,)), ((), ())), preferred_element_type=F32)


def _mm(a, b):
    return jnp.dot(a, b, preferred_element_type=F32)


def _layer_kernel(x_ref, pos_ref, w_in_t_f32, w_up_f32, bg_ref, nw_ref, w_out_f32,
                  lng_ref, lnb_ref, invf_ref, sinks_ref,
                  o_ref, kprev_ref, vprev_ref, s_ref,
                  w_in_ref, w_r_ref, w_up_ref, w_out_ref):
    t_idx = pl.program_id(1)

    @pl.when((pl.program_id(0) == 0) & (t_idx == 0))
    def _cast_weights():
        def cast_group(g, carry):
            rows = pl.ds(pl.multiple_of(g * LANES, LANES), LANES)
            w_in_ref[g] = w_in_t_f32[rows, :].T.astype(BF16)
            return carry
        lax.fori_loop(0, RB // LANES, cast_group, 0)
        rank_rows = jnp.concatenate(
            [w_in_t_f32[RB:, :], jnp.zeros((LANES - GLA_RANK, D_MODEL), F32)], axis=0)
        w_r_ref[...] = rank_rows.T.astype(BF16)

        def cast_rows(r, carry):
            rows = pl.ds(pl.multiple_of(r * W_CAST_ROWS, W_CAST_ROWS), W_CAST_ROWS)
            w_out_ref[rows, :] = w_out_f32[rows, :].astype(BF16)
            return carry
        lax.fori_loop(0, D_MODEL // W_CAST_ROWS, cast_rows, 0)
        w_up_ref[...] = jnp.concatenate(
            [w_up_f32[...], jnp.zeros((LANES - GLA_RANK, GLA_HEADS * GLA_DK), F32)],
            axis=0).astype(BF16)

    @pl.when(t_idx == 0)
    def _reset_carries():
        kprev_ref[...] = jnp.zeros_like(kprev_ref)
        vprev_ref[...] = jnp.zeros_like(vprev_ref)
        s_ref[...] = jnp.zeros_like(s_ref)

    xb = x_ref[...].astype(BF16)

    def proj(lo, width=LANES, rows=slice(None)):
        if lo == RB:
            return _mm(xb[rows], w_r_ref[...])
        groups = [w_in_ref[g] for g in range(lo // LANES, (lo + width) // LANES)]
        return _mm(xb[rows], jnp.concatenate(groups, axis=1))

    lane = lax.broadcasted_iota(jnp.int32, (1, LANES), 1)
    lo_half = lane < HALF
    n_blk = TILE // BLOCK
    n_chunk = TILE // GLA_CHUNK
    n_pair = GLA_HEADS // 2
    pair_w = 2 * GLA_DV

    kv_new = proj(KA, 2 * LANES)
    q_a = proj(QA, SWA_WIDTH)
    r_b = proj(RB).astype(BF16)
    logit = _mm(r_b, w_up_ref[...]) + bg_ref[...]
    log_a = jax.nn.log_sigmoid(logit) / GLA_TAU
    qk_b = proj(QB, 2 * GLA_HEADS * GLA_DK)
    q_b = qk_b[:, :GLA_HEADS * GLA_DK]
    k_b = qk_b[:, GLA_HEADS * GLA_DK:]

    half = ROT_DIM // 2
    pos = pos_ref[...].astype(F32)
    ones_r = jnp.ones((HALF - ROT_DIM, BLOCK), F32)
    zeros_r = jnp.zeros((HALF - ROT_DIM, BLOCK), F32)
    zeros_h = jnp.zeros((half, BLOCK), F32)
    cos_blk, up_blk, dn_blk = [], [], []
    for i in range(TILE // BLOCK):
        ang = invf_ref[...] * pos[:, i * BLOCK:(i + 1) * BLOCK]
        c8, s8 = jnp.cos(ang), jnp.sin(ang)
        cos_blk.append(jnp.concatenate([c8, c8, ones_r] * 2, axis=0).T)
        up_blk.append(jnp.concatenate([zeros_h, s8, zeros_r] * 2, axis=0).T)
        dn_blk.append(jnp.concatenate([-s8, zeros_h, zeros_r] * 2, axis=0).T)
    cos = jnp.concatenate(cos_blk, axis=0)
    sin_up = jnp.concatenate(up_blk, axis=0)
    sin_dn = jnp.concatenate(dn_blk, axis=0)

    def rope(t):
        return (t * cos + pltpu.roll(t, ROT_DIM // 2, 1) * sin_up
                + pltpu.roll(t, LANES - ROT_DIM // 2, 1) * sin_dn)

    k_new = rope(kv_new[:, :LANES])
    v_new = kv_new[:, LANES:]
    k_cat = jnp.concatenate([kprev_ref[...], k_new], axis=0)
    v_cat = jnp.concatenate([vprev_ref[...], v_new], axis=0)
    kprev_ref[...] = k_new[TILE - BLOCK:]
    vprev_ref[...] = v_new[TILE - BLOCK:]

    k_sw = pltpu.roll(k_cat, HALF, 1)
    k_cb, k_sb = k_cat.astype(BF16), k_sw.astype(BF16)
    k_zero = jnp.zeros_like(k_cb)
    k_even = [jnp.where(lo_half, k_cb, k_zero), jnp.where(lo_half, k_sb, k_zero)]
    k_odd = [jnp.where(lo_half, k_zero, k_sb), jnp.where(lo_half, k_zero, k_cb)]
    v_t = v_cat.T.astype(BF16)

    scale = SWA_HEAD_DIM ** -0.5
    q_r = [(rope(q_a[:, cg * LANES:(cg + 1) * LANES]) * scale).astype(BF16)
           for cg in range(SWA_WIDTH // LANES)]

    swa_tiles = [(h, i) for i in range(n_blk) for h in range(SWA_KV_HEADS)]
    scores = {}
    for h, i in swa_tiles:
        rows = slice(i * BLOCK, (i + 1) * BLOCK)
        kv_rows = slice(i * BLOCK, (i + 2) * BLOCK)
        q_stack = jnp.concatenate([q_r[2 * h][rows], q_r[2 * h + 1][rows]], axis=0)
        scores[h, i] = [_nt(k_even[h][kv_rows], q_stack),
                        _nt(k_odd[h][kv_rows], q_stack)]

    kj = lax.broadcasted_iota(jnp.int32, (2 * BLOCK, BLOCK), 0)
    qj = lax.broadcasted_iota(jnp.int32, (2 * BLOCK, BLOCK), 1)
    in_window = (kj > qj) & (kj <= qj + WINDOW)
    first_blk_mask = in_window & ((kj >= BLOCK) | (t_idx > 0))

    def softmax_block(i):
        mask = first_blk_mask if i == 0 else in_window
        out = []
        for h in range(SWA_KV_HEADS):
            p_t, inv = [], []
            for g in range(4):
                sink = sinks_ref[4 * h + g]
                s = scores[h, i][g % 2][:, (g // 2) * BLOCK:(g // 2 + 1) * BLOCK]
                s = jnp.where(mask, s, -jnp.inf)
                m = jnp.maximum(jnp.max(s, axis=0, keepdims=True), sink)
                e = jnp.exp(s - m)
                denom = jnp.sum(e, axis=0, keepdims=True) + jnp.exp(sink - m)
                inv.append(1.0 / denom)
                p_t.append(e.astype(BF16))
            out.append((jnp.concatenate(p_t, axis=1), inv))
        return out

    sm = [softmax_block(i) for i in range(n_blk)]

    ri = lax.broadcasted_iota(jnp.int32, (CUM_ROWS, CUM_ROWS), 0)
    ci = lax.broadcasted_iota(jnp.int32, (CUM_ROWS, CUM_ROWS), 1)
    tri = ((ri // GLA_CHUNK == ci // GLA_CHUNK) & (ci <= ri)).astype(BF16)
    la_hi = log_a.astype(BF16)
    la_lo = (log_a - la_hi.astype(F32)).astype(BF16)
    b = jnp.concatenate(
        [_mm(tri, la_hi[r:r + CUM_ROWS]) + _mm(tri, la_lo[r:r + CUM_ROWS])
         for r in range(0, TILE, CUM_ROWS)], axis=0)
    v_b = proj(VB, GLA_WIDTH).astype(BF16)
    gate_a_pre = proj(GA, SWA_WIDTH)

    q_dec = (q_b * (GLA_DK ** -0.5) * jnp.exp(b)).astype(BF16)
    k_inv = (k_b * jnp.exp(-b)).astype(BF16)
    causal2 = (lax.broadcasted_iota(jnp.int32, (GLA_CHUNK, LANES), 1) & (HALF - 1)
               ) <= lax.broadcasted_iota(jnp.int32, (GLA_CHUNK, LANES), 0)
    eye = (lax.broadcasted_iota(jnp.int32, (LANES, LANES), 0)
           == lax.broadcasted_iota(jnp.int32, (LANES, LANES), 1))
    gla_tiles = [(p, c) for c in range(n_chunk) for p in range(n_pair)]
    attn, upd, decay_col = {}, {}, {}
    for p, c in gla_tiles:
        cols = slice(p * LANES, (p + 1) * LANES)
        rows = slice(c * GLA_CHUNK, (c + 1) * GLA_CHUNK)
        last = (c + 1) * GLA_CHUNK - 1
        b_last = b[last:last + 1, cols]
        ki2 = k_inv[rows, cols]
        k_end = (k_b[rows, cols] * jnp.exp(b_last - b[rows, cols])).astype(BF16)
        zero = jnp.zeros_like(ki2)
        k_stack = jnp.concatenate([jnp.where(lo_half, ki2, zero),
                                   jnp.where(lo_half, zero, ki2)], axis=0)
        attn[p, c] = jnp.where(causal2, _nt(q_dec[rows, cols], k_stack), 0.0).astype(BF16)
        u = _tn(k_end, v_b[rows, p * pair_w:(p + 1) * pair_w])
        upd[p, c] = jnp.concatenate([u[:GLA_DK, :GLA_DV], u[GLA_DK:, GLA_DV:]], axis=0)
        decay_row = jnp.broadcast_to(jnp.exp(b_last), (LANES, LANES))
        decay_col[p, c] = jnp.sum(jnp.where(eye, decay_row, 0.0), axis=1, keepdims=True)

    states = {}
    for p in range(n_pair):
        state = s_ref[p * LANES:(p + 1) * LANES, :]
        for c in range(n_chunk):
            states[p, c] = state.astype(BF16)
            state = state * decay_col[p, c] + upd[p, c]
        s_ref[p * LANES:(p + 1) * LANES, :] = state

    zero_v = jnp.zeros((GLA_CHUNK, GLA_DV), BF16)
    norm_w = nw_ref[...]
    chunks_per_blk = BLOCK // GLA_CHUNK

    def gate_b_block(i):
        return jax.nn.silu(proj(GB, GLA_WIDTH, slice(i * BLOCK, (i + 1) * BLOCK)))

    def value_dots_block(i, sm):
        kv_rows = slice(i * BLOCK, (i + 2) * BLOCK)
        o_swa = [_mm(v_t[h * HALF:(h + 1) * HALF, kv_rows], sm[h][0])
                 for h in range(SWA_KV_HEADS)]
        o_gla = {}
        for cc in range(chunks_per_blk):
            c = i * chunks_per_blk + cc
            crows = slice(c * GLA_CHUNK, (c + 1) * GLA_CHUNK)
            for p in range(n_pair):
                v2 = v_b[crows, p * pair_w:(p + 1) * pair_w]
                st = states[p, c]
                rhs = jnp.concatenate([
                    jnp.concatenate([v2[:, :GLA_DV], zero_v], axis=1),
                    jnp.concatenate([zero_v, v2[:, GLA_DV:]], axis=1),
                    jnp.concatenate([st[:GLA_DK], zero_v], axis=1),
                    jnp.concatenate([zero_v, st[GLA_DK:]], axis=1)], axis=0)
                lhs = jnp.concatenate([attn[p, c], q_dec[crows, p * LANES:(p + 1) * LANES]], axis=1)
                o_gla[p, cc] = _mm(lhs, rhs)
        return o_swa, o_gla

    def gated_outputs_block(i, sm, dots, gate_b):
        o_swa, o_gla = dots
        gate_a = jax.nn.silu(gate_a_pre[i * BLOCK:(i + 1) * BLOCK])
        mix = [None] * (D_MODEL // LANES)
        for h in range(SWA_KV_HEADS):
            inv = sm[h][1]
            for c in range(2):
                cg = 2 * h + c
                o2_t = jnp.concatenate(
                    [o_swa[h][:, (2 * c + par) * BLOCK:(2 * c + par + 1) * BLOCK] * inv[2 * c + par]
                     for par in range(2)], axis=0)
                mix[cg] = (o2_t.T * gate_a[:, cg * LANES:(cg + 1) * LANES]).astype(BF16)
        for head in range(GLA_HEADS):
            p, par = divmod(head, 2)
            pieces = []
            for cc in range(chunks_per_blk):
                o = o_gla[p, cc][:, par * GLA_DV:(par + 1) * GLA_DV]
                o = o * lax.rsqrt(jnp.mean(jnp.square(o), axis=-1, keepdims=True) + EPS) * norm_w
                g_rows = slice(cc * GLA_CHUNK, (cc + 1) * GLA_CHUNK)
                pieces.append((o * gate_b[g_rows, head * GLA_DV:(head + 1) * GLA_DV]).astype(BF16))
            mix[SWA_WIDTH // LANES + head] = jnp.concatenate(pieces, axis=0)
        return jnp.concatenate(mix, axis=1)

    def output_block(i, mix):
        rows = slice(i * BLOCK, (i + 1) * BLOCK)
        y = _mm(mix, w_out_ref[...]) + ALPHA * x_ref[rows, :]
        mu = jnp.mean(y, axis=-1, keepdims=True)
        yc = y - mu
        var = jnp.mean(jnp.square(yc), axis=-1, keepdims=True)
        o_ref[rows, :] = yc * lax.rsqrt(var + EPS) * lng_ref[...] + lnb_ref[...]

    gb = {0: gate_b_block(0)}
    dots = {0: value_dots_block(0, sm[0])}
    for i in range(n_blk):
        if i + 1 < n_blk:
            gb[i + 1] = gate_b_block(i + 1)
            dots[i + 1] = value_dots_block(i + 1, sm[i + 1])
        output_block(i, gated_outputs_block(i, sm[i], dots[i], gb[i]))


def _rope_freq_rows():
    half = ROT_DIM // 2
    inv_freq = ROPE_THETA ** (-jnp.arange(half, dtype=F32) / half)
    return jnp.broadcast_to(inv_freq[:, None], (half, BLOCK))


@jax.jit
def kernel(x, positions, w_in, gla_w_gate_up, gla_b_gate, attn_sinks, gla_norm_w, w_out, ln_g, ln_b):
    B, S, D = x.shape
    assert D == D_MODEL and S % TILE == 0 and w_in.shape == (DEPTH, D_MODEL, D_IN_PROJ)
    n_tiles = S // TILE
    pos_rows = positions.reshape(B * n_tiles, 1, TILE)

    def const(shape):
        return pl.BlockSpec((None,) + shape, lambda b, t: (0,) * (len(shape) + 1),
                            pipeline_mode=pl.Buffered(1))

    grid_spec = pl.GridSpec(
        grid=(B, n_tiles),
        in_specs=[
            pl.BlockSpec((None, TILE, D_MODEL), lambda b, t: (b, t, 0)),
            pl.BlockSpec((None, 1, TILE), lambda b, t: (b * n_tiles + t, 0, 0)),
            const((D_IN_PROJ, D_MODEL)),
            const((GLA_RANK, GLA_HEADS * GLA_DK)),
            const((1, GLA_HEADS * GLA_DK)),
            const((1, GLA_DV)),
            const((D_MODEL, D_MODEL)),
            const((1, D_MODEL)),
            const((1, D_MODEL)),
            pl.BlockSpec((ROT_DIM // 2, BLOCK), lambda b, t: (0, 0)),
            pl.BlockSpec(memory_space=pltpu.SMEM),
        ],
        out_specs=pl.BlockSpec((None, TILE, D_MODEL), lambda b, t: (b, t, 0)),
        scratch_shapes=[
            pltpu.VMEM((BLOCK, LANES), F32),
            pltpu.VMEM((BLOCK, LANES), F32),
            pltpu.VMEM((GLA_HEADS * GLA_DK, GLA_DV), F32),
            pltpu.VMEM((RB // LANES, D_MODEL, LANES), BF16),
            pltpu.VMEM((D_MODEL, LANES), BF16),
            pltpu.VMEM((LANES, GLA_HEADS * GLA_DK), BF16),
            pltpu.VMEM((D_MODEL, D_MODEL), BF16),
        ],
    )
    row = lambda a: a.reshape(DEPTH, 1, -1)
    return pl.pallas_call(
        _layer_kernel,
        grid_spec=grid_spec,
        out_shape=jax.ShapeDtypeStruct((B, S, D_MODEL), x.dtype),
        compiler_params=pltpu.CompilerParams(
            dimension_semantics=("arbitrary", "arbitrary"),
            vmem_limit_bytes=VMEM_LIMIT_BYTES),
        name="hymba_layer",
    )(x, pos_rows, jnp.swapaxes(w_in, 1, 2), gla_w_gate_up, row(gla_b_gate), row(gla_norm_w),
      w_out, row(ln_g), row(ln_b), _rope_freq_rows(), attn_sinks[0])
```

```python
import jax
import jax.numpy as jnp
from jax import lax
from jax.experimental import pallas as pl
from jax.experimental.pallas import tpu as pltpu

D_MODEL = 1024
SWA_WIDTH = 512
SWA_HEAD_DIM = 64
SWA_Q_HEADS = 8
SWA_KV_HEADS = 2
WINDOW = 128
BLOCK = 128
ROPE_THETA = 500000.0
ROT_DIM = 16
GLA_WIDTH = 512
GLA_HEADS = 4
GLA_DK = 64
GLA_DV = 128
GLA_RANK = 16
GLA_TAU = 16.0
GLA_CHUNK = 64
EPS = 1e-5
DEPTH = 1
ALPHA = (2 * DEPTH) ** 0.25

LANES = 128
HALF = LANES // 2

QA, KA, VA, GA, QB, KB, VB, GB, RB = 0, 512, 640, 768, 1280, 1536, 1792, 2304, 2816
D_IN_PROJ = RB + GLA_RANK

TILE = 512
CUM_ROWS = 256
OUT_ROWS = 256
W_CAST_ROWS = 128
VMEM_LIMIT_BYTES = 60 * 1024 * 1024

F32 = jnp.float32
BF16 = jnp.bfloat16


def _nt(a, b):
    return lax.dot_general(a, b, (((1,), (1,)), ((), ())), preferred_element_type=F32)


def _tn(a, b):
    return lax.dot_general(a, b, (((0,), (0,)), ((), ())), preferred_element_type=F32)


def _mm(a, b):
    return jnp.dot(a, b, preferred_element_type=F32)


def _layer_kernel(x_ref, pos_ref, w_in_t_f32, w_up_f32, bg_ref, nw_ref, w_out_f32,
                  lng_ref, lnb_ref, invf_ref, sinks_ref,
                  o_ref, kprev_ref, vprev_ref, s_ref, mix_ref,
                  w_in_ref, w_r_ref, w_up_ref, w_out_ref, tri_ref):
    t_idx = pl.program_id(1)

    @pl.when((pl.program_id(0) == 0) & (t_idx == 0))
    def _cast_weights():
        def cast_group(g, carry):
            rows = pl.ds(pl.multiple_of(g * LANES, LANES), LANES)
            w_in_ref[g] = w_in_t_f32[rows, :].T.astype(BF16)
            return carry
        lax.fori_loop(0, RB // LANES, cast_group, 0)
        rank_rows = jnp.concatenate(
            [w_in_t_f32[RB:, :], jnp.zeros((LANES - GLA_RANK, D_MODEL), F32)], axis=0)
        w_r_ref[...] = rank_rows.T.astype(BF16)
        ri = lax.broadcasted_iota(jnp.int32, (CUM_ROWS, CUM_ROWS), 0)
        ci = lax.broadcasted_iota(jnp.int32, (CUM_ROWS, CUM_ROWS), 1)
        tri_ref[...] = ((ri // GLA_CHUNK == ci // GLA_CHUNK) & (ci <= ri)).astype(BF16)

        def cast_rows(r, carry):
            rows = pl.ds(pl.multiple_of(r * W_CAST_ROWS, W_CAST_ROWS), W_CAST_ROWS)
            w_out_ref[rows, :] = w_out_f32[rows, :].astype(BF16)
            return carry
        lax.fori_loop(0, D_MODEL // W_CAST_ROWS, cast_rows, 0)
        w_up_ref[...] = jnp.concatenate(
            [w_up_f32[...], jnp.zeros((LANES - GLA_RANK, GLA_HEADS * GLA_DK), F32)],
            axis=0).astype(BF16)

    @pl.when(t_idx == 0)
    def _reset_carries():
        kprev_ref[...] = jnp.zeros_like(kprev_ref)
        vprev_ref[...] = jnp.zeros_like(vprev_ref)
        s_ref[...] = jnp.zeros_like(s_ref)

    xb = x_ref[...].astype(BF16)

    def proj(lo, width):
        groups = [w_in_ref[g] for g in range(lo // LANES, (lo + width) // LANES)]
        return _mm(xb, jnp.concatenate(groups, axis=1))

    lane = lax.broadcasted_iota(jnp.int32, (1, LANES), 1)
    lo_half = lane < HALF
    n_blk = TILE // BLOCK
    n_chunk = TILE // GLA_CHUNK
    n_pair = GLA_HEADS // 2
    pair_w = 2 * GLA_DV

    r_b = _mm(xb, w_r_ref[...]).astype(BF16)
    kv_new = proj(KA, 2 * LANES)
    q_a = proj(QA, SWA_WIDTH)
    logit = _mm(r_b, w_up_ref[...]) + bg_ref[...]
    log_a = jax.nn.log_sigmoid(logit) / GLA_TAU
    qk_b = proj(QB, 2 * GLA_HEADS * GLA_DK)
    q_b = qk_b[:, :GLA_HEADS * GLA_DK]
    k_b = qk_b[:, GLA_HEADS * GLA_DK:]

    tri = tri_ref[...]
    la_hi = log_a.astype(BF16)
    la_lo = (log_a - la_hi.astype(F32)).astype(BF16)
    b = jnp.concatenate(
        [_mm(tri, la_hi[r:r + CUM_ROWS]) + _mm(tri, la_lo[r:r + CUM_ROWS])
         for r in range(0, TILE, CUM_ROWS)], axis=0)
    v_b = proj(VB, GLA_WIDTH).astype(BF16)

    half = ROT_DIM // 2
    pos = pos_ref[...].astype(F32)
    ones_r = jnp.ones((HALF - ROT_DIM, BLOCK), F32)
    zeros_r = jnp.zeros((HALF - ROT_DIM, BLOCK), F32)
    zeros_h = jnp.zeros((half, BLOCK), F32)
    cos_blk, up_blk, dn_blk = [], [], []
    for i in range(TILE // BLOCK):
        ang = invf_ref[...] * pos[:, i * BLOCK:(i + 1) * BLOCK]
        c8, s8 = jnp.cos(ang), jnp.sin(ang)
        cos_blk.append(jnp.concatenate([c8, c8, ones_r] * 2, axis=0).T)
        up_blk.append(jnp.concatenate([zeros_h, s8, zeros_r] * 2, axis=0).T)
        dn_blk.append(jnp.concatenate([-s8, zeros_h, zeros_r] * 2, axis=0).T)
    cos = jnp.concatenate(cos_blk, axis=0)
    sin_up = jnp.concatenate(up_blk, axis=0)
    sin_dn = jnp.concatenate(dn_blk, axis=0)

    def rope(t):
        return (t * cos + pltpu.roll(t, ROT_DIM // 2, 1) * sin_up
                + pltpu.roll(t, LANES - ROT_DIM // 2, 1) * sin_dn)

    k_new = rope(kv_new[:, :LANES])
    v_new = kv_new[:, LANES:]
    k_cat = jnp.concatenate([kprev_ref[...], k_new], axis=0)
    v_cat = jnp.concatenate([vprev_ref[...], v_new], axis=0)
    kprev_ref[...] = k_new[TILE - BLOCK:]
    vprev_ref[...] = v_new[TILE - BLOCK:]

    k_sw = pltpu.roll(k_cat, HALF, 1)
    k_cb, k_sb = k_cat.astype(BF16), k_sw.astype(BF16)
    k_zero = jnp.zeros_like(k_cb)
    k_even = [jnp.where(lo_half, k_cb, k_zero), jnp.where(lo_half, k_sb, k_zero)]
    k_odd = [jnp.where(lo_half, k_zero, k_sb), jnp.where(lo_half, k_zero, k_cb)]
    v_t = v_cat.T.astype(BF16)

    scale = SWA_HEAD_DIM ** -0.5
    q_r = [(rope(q_a[:, cg * LANES:(cg + 1) * LANES]) * scale).astype(BF16)
           for cg in range(SWA_WIDTH // LANES)]

    swa_tiles = [(h, i) for i in range(n_blk) for h in range(SWA_KV_HEADS)]
    scores = {}
    for h, i in swa_tiles:
        rows = slice(i * BLOCK, (i + 1) * BLOCK)
        kv_rows = slice(i * BLOCK, (i + 2) * BLOCK)
        q_stack = jnp.concatenate([q_r[2 * h][rows], q_r[2 * h + 1][rows]], axis=0)
        scores[h, i] = [_nt(k_even[h][kv_rows], q_stack),
                        _nt(k_odd[h][kv_rows], q_stack)]

    kj = lax.broadcasted_iota(jnp.int32, (2 * BLOCK, BLOCK), 0)
    qj = lax.broadcasted_iota(jnp.int32, (2 * BLOCK, BLOCK), 1)
    in_window = (kj > qj) & (kj <= qj + WINDOW)
    first_blk_mask = in_window & ((kj >= BLOCK) | (t_idx > 0))
    probs, inv_den = {}, {}
    for h, i in swa_tiles:
        mask = first_blk_mask if i == 0 else in_window
        p_t, inv = [], []
        for g in range(4):
            sink = sinks_ref[4 * h + g]
            s = scores[h, i][g % 2][:, (g // 2) * BLOCK:(g // 2 + 1) * BLOCK]
            s = jnp.where(mask, s, -jnp.inf)
            m = jnp.maximum(jnp.max(s, axis=0, keepdims=True), sink)
            e = jnp.exp(s - m)
            denom = jnp.sum(e, axis=0, keepdims=True) + jnp.exp(sink - m)
            inv.append(1.0 / denom)
            p_t.append(e.astype(BF16))
        probs[h, i] = jnp.concatenate(p_t, axis=1)
        inv_den[h, i] = inv

    q_dec = (q_b * (GLA_DK ** -0.5) * jnp.exp(b)).astype(BF16)
    k_inv = (k_b * jnp.exp(-b)).astype(BF16)
    causal2 = (lax.broadcasted_iota(jnp.int32, (GLA_CHUNK, LANES), 1) & (HALF - 1)
               ) <= lax.broadcasted_iota(jnp.int32, (GLA_CHUNK, LANES), 0)
    eye = (lax.broadcasted_iota(jnp.int32, (LANES, LANES), 0)
           == lax.broadcasted_iota(jnp.int32, (LANES, LANES), 1))
    gla_tiles = [(p, c) for c in range(n_chunk) for p in range(n_pair)]
    b_end = jnp.concatenate(
        [jnp.broadcast_to(b[(c + 1) * GLA_CHUNK - 1:(c + 1) * GLA_CHUNK, :],
                          (GLA_CHUNK, GLA_HEADS * GLA_DK)) for c in range(n_chunk)], axis=0)
    k_end_t = (k_b * jnp.exp(b_end - b)).T.astype(BF16)
    v_zero = jnp.zeros((GLA_CHUNK, pair_w), BF16)
    attn, upd, decay_col = {}, {}, {}
    for p, c in gla_tiles:
        cols = slice(p * LANES, (p + 1) * LANES)
        blk = slice((c * GLA_CHUNK // BLOCK) * BLOCK, (c * GLA_CHUNK // BLOCK + 1) * BLOCK)
        v2 = v_b[c * GLA_CHUNK:(c + 1) * GLA_CHUNK, p * pair_w:(p + 1) * pair_w]
        v_pad = jnp.concatenate([v2, v_zero] if (c * GLA_CHUNK) % BLOCK == 0 else [v_zero, v2], axis=0)
        u = _mm(k_end_t[cols, blk], v_pad)
        upd[p, c] = jnp.concatenate([u[:GLA_DK, :GLA_DV], u[GLA_DK:, GLA_DV:]], axis=0)
    for p, c in gla_tiles:
        cols = slice(p * LANES, (p + 1) * LANES)
        rows = slice(c * GLA_CHUNK, (c + 1) * GLA_CHUNK)
        last = (c + 1) * GLA_CHUNK - 1
        b_last = b[last:last + 1, cols]
        ki2 = k_inv[rows, cols]
        zero = jnp.zeros_like(ki2)
        k_stack = jnp.concatenate([jnp.where(lo_half, ki2, zero),
                                   jnp.where(lo_half, zero, ki2)], axis=0)
        attn[p, c] = jnp.where(causal2, _nt(q_dec[rows, cols], k_stack), 0.0).astype(BF16)
        decay_row = jnp.broadcast_to(jnp.exp(b_last), (LANES, LANES))
        decay_col[p, c] = jnp.sum(jnp.where(eye, decay_row, 0.0), axis=1, keepdims=True)

    gate_a = jax.nn.silu(proj(GA, SWA_WIDTH))
    gate_b = jax.nn.silu(proj(GB, GLA_WIDTH))

    states = {}
    for p in range(n_pair):
        state = s_ref[p * LANES:(p + 1) * LANES, :]
        for c in range(n_chunk):
            states[p, c] = state.astype(BF16)
            state = state * decay_col[p, c] + upd[p, c]
        s_ref[p * LANES:(p + 1) * LANES, :] = state

    o_swa = {}
    for h, i in swa_tiles:
        kv_rows = slice(i * BLOCK, (i + 2) * BLOCK)
        o_swa[h, i] = _mm(v_t[h * HALF:(h + 1) * HALF, kv_rows], probs[h, i])
    zero_v = jnp.zeros((GLA_CHUNK, GLA_DV), BF16)
    o_gla = {}
    for p, c in gla_tiles:
        rows = slice(c * GLA_CHUNK, (c + 1) * GLA_CHUNK)
        v2 = v_b[rows, p * pair_w:(p + 1) * pair_w]
        st = states[p, c]
        rhs = jnp.concatenate([
            jnp.concatenate([v2[:, :GLA_DV], zero_v], axis=1),
            jnp.concatenate([zero_v, v2[:, GLA_DV:]], axis=1),
            jnp.concatenate([st[:GLA_DK], zero_v], axis=1),
            jnp.concatenate([zero_v, st[GLA_DK:]], axis=1)], axis=0)
        lhs = jnp.concatenate([attn[p, c], q_dec[rows, p * LANES:(p + 1) * LANES]], axis=1)
        o_gla[p, c] = _mm(lhs, rhs)

    for h, i in swa_tiles:
        rows = slice(i * BLOCK, (i + 1) * BLOCK)
        o_t, inv = o_swa[h, i], inv_den[h, i]
        for c in range(2):
            cg = 2 * h + c
            o2_t = jnp.concatenate(
                [o_t[:, (2 * c + par) * BLOCK:(2 * c + par + 1) * BLOCK] * inv[2 * c + par]
                 for par in range(2)], axis=0)
            cg_cols = slice(cg * LANES, (cg + 1) * LANES)
            out = o2_t.T * gate_a[rows, cg_cols]
            mix_ref[rows, cg_cols] = out.astype(BF16)
    norm_w = nw_ref[...]
    for p, c in gla_tiles:
        rows = slice(c * GLA_CHUNK, (c + 1) * GLA_CHUNK)
        for par in range(2):
            o = o_gla[p, c][:, par * GLA_DV:(par + 1) * GLA_DV]
            o = o * lax.rsqrt(jnp.mean(jnp.square(o), axis=-1, keepdims=True) + EPS) * norm_w
            c0 = (2 * p + par) * GLA_DV
            out = o * gate_b[rows, c0:c0 + GLA_DV]
            mix_ref[rows, SWA_WIDTH + c0:SWA_WIDTH + c0 + GLA_DV] = out.astype(BF16)

    for r in range(0, TILE, OUT_ROWS):
        rows = slice(r, r + OUT_ROWS)
        y = _mm(mix_ref[rows, :], w_out_ref[...]) + ALPHA * x_ref[rows, :]
        mu = jnp.mean(y, axis=-1, keepdims=True)
        yc = y - mu
        var = jnp.mean(jnp.square(yc), axis=-1, keepdims=True)
        o_ref[rows, :] = yc * lax.rsqrt(var + EPS) * lng_ref[...] + lnb_ref[...]


def _rope_freq_rows():
    half = ROT_DIM // 2
    inv_freq = ROPE_THETA ** (-jnp.arange(half, dtype=F32) / half)
    return jnp.broadcast_to(inv_freq[:, None], (half, BLOCK))


@jax.jit
def kernel(x, positions, w_in, gla_w_gate_up, gla_b_gate, attn_sinks, gla_norm_w, w_out, ln_g, ln_b):
    B, S, D = x.shape
    assert D == D_MODEL and S % TILE == 0 and w_in.shape == (DEPTH, D_MODEL, D_IN_PROJ)
    n_tiles = S // TILE
    pos_rows = positions.reshape(B * n_tiles, 1, TILE)

    def const(shape):
        return pl.BlockSpec((None,) + shape, lambda b, t: (0,) * (len(shape) + 1),
                            pipeline_mode=pl.Buffered(1))

    grid_spec = pl.GridSpec(
        grid=(B, n_tiles),
        in_specs=[
            pl.BlockSpec((None, TILE, D_MODEL), lambda b, t: (b, t, 0)),
            pl.BlockSpec((None, 1, TILE), lambda b, t: (b * n_tiles + t, 0, 0)),
            const((D_IN_PROJ, D_MODEL)),
            const((GLA_RANK, GLA_HEADS * GLA_DK)),
            const((1, GLA_HEADS * GLA_DK)),
            const((1, GLA_DV)),
            const((D_MODEL, D_MODEL)),
            const((1, D_MODEL)),
            const((1, D_MODEL)),
            pl.BlockSpec((ROT_DIM // 2, BLOCK), lambda b, t: (0, 0)),
            pl.BlockSpec(memory_space=pltpu.SMEM),
        ],
        out_specs=pl.BlockSpec((None, TILE, D_MODEL), lambda b, t: (b, t, 0)),
        scratch_shapes=[
            pltpu.VMEM((BLOCK, LANES), F32),
            pltpu.VMEM((BLOCK, LANES), F32),
            pltpu.VMEM((GLA_HEADS * GLA_DK, GLA_DV), F32),
            pltpu.VMEM((TILE, D_MODEL), BF16),
            pltpu.VMEM((RB // LANES, D_MODEL, LANES), BF16),
            pltpu.VMEM((D_MODEL, LANES), BF16),
            pltpu.VMEM((LANES, GLA_HEADS * GLA_DK), BF16),
            pltpu.VMEM((D_MODEL, D_MODEL), BF16),
            pltpu.VMEM((CUM_ROWS, CUM_ROWS), BF16),
        ],
    )
    row = lambda a: a.reshape(DEPTH, 1, -1)
    return pl.pallas_call(
        _layer_kernel,
        grid_spec=grid_spec,
        out_shape=jax.ShapeDtypeStruct((B, S, D_MODEL), x.dtype),
        compiler_params=pltpu.CompilerParams(
            dimension_semantics=("arbitrary", "arbitrary"),
            vmem_limit_bytes=VMEM_LIMIT_BYTES),
        name="hymba_layer",
    )(x, pos_rows, jnp.swapaxes(w_in, 1, 2), gla_w_gate_up, row(gla_b_gate), row(gla_norm_w),
      w_out, row(ln_g), row(ln_b), _rope_freq_rows(), attn_sinks[0])
```

```python
import jax
import jax.numpy as jnp
from jax import lax
from jax.experimental import pallas as pl
from jax.experimental.pallas import tpu as pltpu

D_MODEL = 1024
SWA_WIDTH = 512
SWA_HEAD_DIM = 64
SWA_Q_HEADS = 8
SWA_KV_HEADS = 2
WINDOW = 128
BLOCK = 128
ROPE_THETA = 500000.0
ROT_DIM = 16
GLA_WIDTH = 512
GLA_HEADS = 4
GLA_DK = 64
GLA_DV = 128
GLA_RANK = 16
GLA_TAU = 16.0
GLA_CHUNK = 64
EPS = 1e-5
DEPTH = 1
ALPHA = (2 * DEPTH) ** 0.25

LANES = 128
HALF = LANES // 2

QA, KA, VA, GA, QB, KB, VB, GB, RB = 0, 512, 640, 768, 1280, 1536, 1792, 2304, 2816
D_IN_PROJ = RB + GLA_RANK

TILE = 512
SUB = 256
W_CAST_ROWS = 128
VMEM_LIMIT_BYTES = 60 * 1024 * 1024

F32 = jnp.float32
BF16 = jnp.bfloat16


def _nt(a, b):
    return lax.dot_general(a, b, (((1,), (1,)), ((), ())), preferred_element_type=F32)


def _mm(a, b):
    return jnp.dot(a, b, preferred_element_type=F32)


def _layer_kernel(x_ref, pos_ref, w_in_t_f32, w_up_f32, bg_ref, nw_ref, w_out_f32,
                  lng_ref, lnb_ref, invf_ref, sinks_ref,
                  o_ref, kprev_ref, vprev_ref, s_ref,
                  w_in_ref, w_r_ref, w_up_ref, w_out_ref, tri_ref):
    t_idx = pl.program_id(1)

    @pl.when((pl.program_id(0) == 0) & (t_idx == 0))
    def _cast_weights():
        def cast_group(g, carry):
            rows = pl.ds(pl.multiple_of(g * LANES, LANES), LANES)
            w_in_ref[g] = w_in_t_f32[rows, :].T.astype(BF16)
            return carry
        lax.fori_loop(0, RB // LANES, cast_group, 0)
        rank_rows = jnp.concatenate(
            [w_in_t_f32[RB:, :], jnp.zeros((LANES - GLA_RANK, D_MODEL), F32)], axis=0)
        w_r_ref[...] = rank_rows.T.astype(BF16)
        ri = lax.broadcasted_iota(jnp.int32, (SUB, SUB), 0)
        ci = lax.broadcasted_iota(jnp.int32, (SUB, SUB), 1)
        tri_ref[...] = ((ri // GLA_CHUNK == ci // GLA_CHUNK) & (ci <= ri)).astype(BF16)

        def cast_rows(r, carry):
            rows = pl.ds(pl.multiple_of(r * W_CAST_ROWS, W_CAST_ROWS), W_CAST_ROWS)
            w_out_ref[rows, :] = w_out_f32[rows, :].astype(BF16)
            return carry
        lax.fori_loop(0, D_MODEL // W_CAST_ROWS, cast_rows, 0)
        w_up_ref[...] = jnp.concatenate(
            [w_up_f32[...], jnp.zeros((LANES - GLA_RANK, GLA_HEADS * GLA_DK), F32)],
            axis=0).astype(BF16)

    @pl.when(t_idx == 0)
    def _reset_carries():
        kprev_ref[...] = jnp.zeros_like(kprev_ref)
        vprev_ref[...] = jnp.zeros_like(vprev_ref)
        s_ref[...] = jnp.zeros_like(s_ref)

    n_sub = TILE // SUB
    n_blk = SUB // BLOCK
    n_chunk = SUB // GLA_CHUNK
    n_pair = GLA_HEADS // 2
    pair_w = 2 * GLA_DV
    swa_tiles = [(h, i) for i in range(n_blk) for h in range(SWA_KV_HEADS)]
    gla_tiles = [(p, c) for c in range(n_chunk) for p in range(n_pair)]

    lane = lax.broadcasted_iota(jnp.int32, (1, LANES), 1)
    lo_half = lane < HALF
    kj = lax.broadcasted_iota(jnp.int32, (2 * BLOCK, BLOCK), 0)
    qj = lax.broadcasted_iota(jnp.int32, (2 * BLOCK, BLOCK), 1)
    in_window = (kj > qj) & (kj <= qj + WINDOW)
    first_blk_mask = in_window & ((kj >= BLOCK) | (t_idx > 0))
    gla_lanes = GLA_HEADS * GLA_DK
    head_of_lane = lax.broadcasted_iota(jnp.int32, (1, gla_lanes), 1) // GLA_DK
    causal4 = (lax.broadcasted_iota(jnp.int32, (GLA_CHUNK, gla_lanes), 1) & (GLA_DK - 1)
               ) <= lax.broadcasted_iota(jnp.int32, (GLA_CHUNK, gla_lanes), 0)
    eye = (lax.broadcasted_iota(jnp.int32, (LANES, LANES), 0)
           == lax.broadcasted_iota(jnp.int32, (LANES, LANES), 1))
    v_zero = jnp.zeros((GLA_CHUNK, pair_w), BF16)
    zero_v = jnp.zeros((GLA_CHUNK, GLA_DV), BF16)
    norm_w = nw_ref[...]

    def sub_rows(j):
        return slice(j * SUB, (j + 1) * SUB)

    def proj(xb, lo, width):
        groups = [w_in_ref[g] for g in range(lo // LANES, (lo + width) // LANES)]
        return _mm(xb, jnp.concatenate(groups, axis=1))

    def projections(j):
        xb = x_ref[sub_rows(j), :].astype(BF16)
        d = {"xb": xb}
        r_b = _mm(xb, w_r_ref[...]).astype(BF16)
        d["kv"] = proj(xb, KA, 2 * LANES)
        d["q_a"] = proj(xb, QA, SWA_WIDTH)
        logit = _mm(r_b, w_up_ref[...]) + bg_ref[...]
        log_a = jax.nn.log_sigmoid(logit) / GLA_TAU
        qk_b = proj(xb, QB, 2 * GLA_HEADS * GLA_DK)
        d["q_b"] = qk_b[:, :GLA_HEADS * GLA_DK]
        d["k_b"] = qk_b[:, GLA_HEADS * GLA_DK:]
        la_hi = log_a.astype(BF16)
        la_lo = (log_a - la_hi.astype(F32)).astype(BF16)
        cum = _mm(tri_ref[...], jnp.concatenate([la_hi, la_lo], axis=1))
        d["b"] = cum[:, :GLA_HEADS * GLA_DK] + cum[:, GLA_HEADS * GLA_DK:]
        d["v_b"] = proj(xb, VB, GLA_WIDTH).astype(BF16)
        return d

    def rotary_and_scores(j, d, k_prev, v_prev):
        half = ROT_DIM // 2
        pos = pos_ref[:, sub_rows(j)].astype(F32)
        ones_r = jnp.ones((HALF - ROT_DIM, BLOCK), F32)
        zeros_r = jnp.zeros((HALF - ROT_DIM, BLOCK), F32)
        zeros_h = jnp.zeros((half, BLOCK), F32)
        cos_blk, up_blk, dn_blk = [], [], []
        for i in range(n_blk):
            ang = invf_ref[...] * pos[:, i * BLOCK:(i + 1) * BLOCK]
            c8, s8 = jnp.cos(ang), jnp.sin(ang)
            cos_blk.append(jnp.concatenate([c8, c8, ones_r] * 2, axis=0).T)
            up_blk.append(jnp.concatenate([zeros_h, s8, zeros_r] * 2, axis=0).T)
            dn_blk.append(jnp.concatenate([-s8, zeros_h, zeros_r] * 2, axis=0).T)
        cos = jnp.concatenate(cos_blk, axis=0)
        sin_up = jnp.concatenate(up_blk, axis=0)
        sin_dn = jnp.concatenate(dn_blk, axis=0)

        def rope(t):
            return (t * cos + pltpu.roll(t, ROT_DIM // 2, 1) * sin_up
                    + pltpu.roll(t, LANES - ROT_DIM // 2, 1) * sin_dn)

        k_new = rope(d["kv"][:, :LANES])
        v_new = d["kv"][:, LANES:]
        k_cat = jnp.concatenate([k_prev, k_new], axis=0)
        v_cat = jnp.concatenate([v_prev, v_new], axis=0)
        d["k_last"] = k_new[SUB - BLOCK:]
        d["v_last"] = v_new[SUB - BLOCK:]

        k_sw = pltpu.roll(k_cat, HALF, 1)
        k_cb, k_sb = k_cat.astype(BF16), k_sw.astype(BF16)
        k_zero = jnp.zeros_like(k_cb)
        k_even = [jnp.where(lo_half, k_cb, k_zero), jnp.where(lo_half, k_sb, k_zero)]
        k_odd = [jnp.where(lo_half, k_zero, k_sb), jnp.where(lo_half, k_zero, k_cb)]
        d["v_t"] = v_cat.T.astype(BF16)

        scale = SWA_HEAD_DIM ** -0.5
        q_r = [(rope(d["q_a"][:, cg * LANES:(cg + 1) * LANES]) * scale).astype(BF16)
               for cg in range(SWA_WIDTH // LANES)]

        scores = {}
        for h, i in swa_tiles:
            rows = slice(i * BLOCK, (i + 1) * BLOCK)
            kv_rows = slice(i * BLOCK, (i + 2) * BLOCK)
            q_stack = jnp.concatenate([q_r[2 * h][rows], q_r[2 * h + 1][rows]], axis=0)
            s_all = _nt(jnp.concatenate([k_even[h][kv_rows], k_odd[h][kv_rows]], axis=0), q_stack)
            scores[h, i] = [s_all[:2 * BLOCK], s_all[2 * BLOCK:]]
        d["scores"] = scores

    def gla_dots(d):
        b, k_b = d["b"], d["k_b"]
        d["q_dec"] = (d["q_b"] * (GLA_DK ** -0.5) * jnp.exp(b)).astype(BF16)
        k_inv = (k_b * jnp.exp(-b)).astype(BF16)
        b_end = jnp.concatenate(
            [jnp.broadcast_to(b[(c + 1) * GLA_CHUNK - 1:(c + 1) * GLA_CHUNK, :],
                              (GLA_CHUNK, GLA_HEADS * GLA_DK)) for c in range(n_chunk)], axis=0)
        k_end_t = (k_b * jnp.exp(b_end - b)).T.astype(BF16)
        attn, upd, decay_col = {}, {}, {}
        for p, c in gla_tiles:
            cols = slice(p * LANES, (p + 1) * LANES)
            blk = slice((c * GLA_CHUNK // BLOCK) * BLOCK, (c * GLA_CHUNK // BLOCK + 1) * BLOCK)
            v2 = d["v_b"][c * GLA_CHUNK:(c + 1) * GLA_CHUNK, p * pair_w:(p + 1) * pair_w]
            v_pad = jnp.concatenate([v2, v_zero] if (c * GLA_CHUNK) % BLOCK == 0 else [v_zero, v2], axis=0)
            u = _mm(k_end_t[cols, blk], v_pad)
            upd[p, c] = jnp.concatenate([u[:GLA_DK, :GLA_DV], u[GLA_DK:, GLA_DV:]], axis=0)
        for c in range(n_chunk):
            rows = slice(c * GLA_CHUNK, (c + 1) * GLA_CHUNK)
            last = (c + 1) * GLA_CHUNK - 1
            ki4 = k_inv[rows, :]
            zero = jnp.zeros_like(ki4)
            k_stack = jnp.concatenate([jnp.where(head_of_lane == hd, ki4, zero)
                                       for hd in range(GLA_HEADS)], axis=0)
            a4 = jnp.where(causal4, _nt(d["q_dec"][rows, :], k_stack), 0.0).astype(BF16)
            for p in range(n_pair):
                cols = slice(p * LANES, (p + 1) * LANES)
                attn[p, c] = a4[:, cols]
                decay_row = jnp.broadcast_to(jnp.exp(b[last:last + 1, cols]), (LANES, LANES))
                decay_col[p, c] = jnp.sum(jnp.where(eye, decay_row, 0.0), axis=1, keepdims=True)
        d["attn"], d["upd"], d["decay_col"] = attn, upd, decay_col

    def gate_projections(d):
        d["gate_a"] = jax.nn.silu(proj(d["xb"], GA, SWA_WIDTH))
        d["gate_b"] = jax.nn.silu(proj(d["xb"], GB, GLA_WIDTH))

    def softmax_and_recurrence(j, d, state):
        probs, inv_den = {}, {}
        for h, i in swa_tiles:
            mask = first_blk_mask if (j == 0 and i == 0) else in_window
            p_t, inv = [], []
            for g in range(4):
                sink = sinks_ref[4 * h + g]
                s = d["scores"][h, i][g % 2][:, (g // 2) * BLOCK:(g // 2 + 1) * BLOCK]
                s = jnp.where(mask, s, -jnp.inf)
                m = jnp.maximum(jnp.max(s, axis=0, keepdims=True), sink)
                e = jnp.exp(s - m)
                denom = jnp.sum(e, axis=0, keepdims=True) + jnp.exp(sink - m)
                inv.append(1.0 / denom)
                p_t.append(e.astype(BF16))
            probs[h, i] = jnp.concatenate(p_t, axis=1)
            inv_den[h, i] = inv
        d["probs"], d["inv_den"] = probs, inv_den
        states = {}
        new_state = []
        for p in range(n_pair):
            st = state[p]
            for c in range(n_chunk):
                states[p, c] = st.astype(BF16)
                st = st * d["decay_col"][p, c] + d["upd"][p, c]
            new_state.append(st)
        d["states"] = states
        return new_state

    def value_dots(d):
        o_swa = {}
        for h, i in swa_tiles:
            kv_rows = slice(i * BLOCK, (i + 2) * BLOCK)
            o_swa[h, i] = _mm(d["v_t"][h * HALF:(h + 1) * HALF, kv_rows], d["probs"][h, i])
        o_gla = {}
        for p, c in gla_tiles:
            rows = slice(c * GLA_CHUNK, (c + 1) * GLA_CHUNK)
            v2 = d["v_b"][rows, p * pair_w:(p + 1) * pair_w]
            st = d["states"][p, c]
            rhs = jnp.concatenate([
                jnp.concatenate([v2[:, :GLA_DV], zero_v], axis=1),
                jnp.concatenate([zero_v, v2[:, GLA_DV:]], axis=1),
                jnp.concatenate([st[:GLA_DK], zero_v], axis=1),
                jnp.concatenate([zero_v, st[GLA_DK:]], axis=1)], axis=0)
            lhs = jnp.concatenate([d["attn"][p, c], d["q_dec"][rows, p * LANES:(p + 1) * LANES]], axis=1)
            o_gla[p, c] = _mm(lhs, rhs)
        d["o_swa"], d["o_gla"] = o_swa, o_gla

    def outputs(j, d):
        mix = [[None] * n_blk for _ in range(D_MODEL // LANES)]
        for h, i in swa_tiles:
            rows = slice(i * BLOCK, (i + 1) * BLOCK)
            o_t, inv = d["o_swa"][h, i], d["inv_den"][h, i]
            for c in range(2):
                cg = 2 * h + c
                o2_t = jnp.concatenate(
                    [o_t[:, (2 * c + par) * BLOCK:(2 * c + par + 1) * BLOCK] * inv[2 * c + par]
                     for par in range(2)], axis=0)
                mix[cg][i] = (o2_t.T * d["gate_a"][rows, cg * LANES:(cg + 1) * LANES]).astype(BF16)
        gla_rows = [[None] * n_chunk for _ in range(GLA_HEADS)]
        for p, c in gla_tiles:
            rows = slice(c * GLA_CHUNK, (c + 1) * GLA_CHUNK)
            for par in range(2):
                head = 2 * p + par
                o = d["o_gla"][p, c][:, par * GLA_DV:(par + 1) * GLA_DV]
                o = o * lax.rsqrt(jnp.mean(jnp.square(o), axis=-1, keepdims=True) + EPS) * norm_w
                gla_rows[head][c] = (o * d["gate_b"][rows, head * GLA_DV:(head + 1) * GLA_DV]).astype(BF16)
        cols = [jnp.concatenate(mix[cg], axis=0) for cg in range(SWA_WIDTH // LANES)]
        cols += [jnp.concatenate(gla_rows[head], axis=0) for head in range(GLA_HEADS)]
        y = _mm(jnp.concatenate(cols, axis=1), w_out_ref[...]) + ALPHA * x_ref[sub_rows(j), :]
        mu = jnp.mean(y, axis=-1, keepdims=True)
        yc = y - mu
        var = jnp.mean(jnp.square(yc), axis=-1, keepdims=True)
        o_ref[sub_rows(j), :] = yc * lax.rsqrt(var + EPS) * lng_ref[...] + lnb_ref[...]

    subs = [None] * n_sub
    subs[0] = projections(0)
    rotary_and_scores(0, subs[0], kprev_ref[...], vprev_ref[...])
    gla_dots(subs[0])
    state = [s_ref[p * LANES:(p + 1) * LANES, :] for p in range(n_pair)]
    for j in range(n_sub):
        d = subs[j]
        if j + 1 < n_sub:
            subs[j + 1] = projections(j + 1)
        gate_projections(d)
        state = softmax_and_recurrence(j, d, state)
        value_dots(d)
        if j + 1 < n_sub:
            rotary_and_scores(j + 1, subs[j + 1], d["k_last"], d["v_last"])
            gla_dots(subs[j + 1])
        outputs(j, d)
        subs[j] = None if j + 1 < n_sub else d
    kprev_ref[...] = subs[n_sub - 1]["k_last"]
    vprev_ref[...] = subs[n_sub - 1]["v_last"]
    for p in range(n_pair):
        s_ref[p * LANES:(p + 1) * LANES, :] = state[p]


def _rope_freq_rows():
    half = ROT_DIM // 2
    inv_freq = ROPE_THETA ** (-jnp.arange(half, dtype=F32) / half)
    return jnp.broadcast_to(inv_freq[:, None], (half, BLOCK))


@jax.jit
def kernel(x, positions, w_in, gla_w_gate_up, gla_b_gate, attn_sinks, gla_norm_w, w_out, ln_g, ln_b):
    B, S, D = x.shape
    assert D == D_MODEL and S % TILE == 0 and w_in.shape == (DEPTH, D_MODEL, D_IN_PROJ)
    n_tiles = S // TILE
    pos_rows = positions.reshape(B * n_tiles, 1, TILE)

    def const(shape):
        return pl.BlockSpec((None,) + shape, lambda b, t: (0,) * (len(shape) + 1),
                            pipeline_mode=pl.Buffered(1))

    grid_spec = pl.GridSpec(
        grid=(B, n_tiles),
        in_specs=[
            pl.BlockSpec((None, TILE, D_MODEL), lambda b, t: (b, t, 0)),
            pl.BlockSpec((None, 1, TILE), lambda b, t: (b * n_tiles + t, 0, 0)),
            const((D_IN_PROJ, D_MODEL)),
            const((GLA_RANK, GLA_HEADS * GLA_DK)),
            const((1, GLA_HEADS * GLA_DK)),
            const((1, GLA_DV)),
            const((D_MODEL, D_MODEL)),
            const((1, D_MODEL)),
            const((1, D_MODEL)),
            pl.BlockSpec((ROT_DIM // 2, BLOCK), lambda b, t: (0, 0)),
            pl.BlockSpec(memory_space=pltpu.SMEM),
        ],
        out_specs=pl.BlockSpec((None, TILE, D_MODEL), lambda b, t: (b, t, 0)),
        scratch_shapes=[
            pltpu.VMEM((BLOCK, LANES), F32),
            pltpu.VMEM((BLOCK, LANES), F32),
            pltpu.VMEM((GLA_HEADS * GLA_DK, GLA_DV), F32),
            pltpu.VMEM((RB // LANES, D_MODEL, LANES), BF16),
            pltpu.VMEM((D_MODEL, LANES), BF16),
            pltpu.VMEM((LANES, GLA_HEADS * GLA_DK), BF16),
            pltpu.VMEM((D_MODEL, D_MODEL), BF16),
            pltpu.VMEM((SUB, SUB), BF16),
        ],
    )
    row = lambda a: a.reshape(DEPTH, 1, -1)
    return pl.pallas_call(
        _layer_kernel,
        grid_spec=grid_spec,
        out_shape=jax.ShapeDtypeStruct((B, S, D_MODEL), x.dtype),
        compiler_params=pltpu.CompilerParams(
            dimension_semantics=("arbitrary", "arbitrary"),
            vmem_limit_bytes=VMEM_LIMIT_BYTES),
        name="hymba_layer",
    )(x, pos_rows, jnp.swapaxes(w_in, 1, 2), gla_w_gate_up, row(gla_b_gate), row(gla_norm_w),
      w_out, row(ln_g), row(ln_b), _rope_freq_rows(), attn_sinks[0])
```

```python
import functools

import jax
import jax.numpy as jnp
from jax import lax
from jax.experimental import pallas as pl
from jax.experimental.pallas import tpu as pltpu

D_MODEL = 1024
SWA_WIDTH = 512
SWA_HEAD_DIM = 64
SWA_Q_HEADS = 8
SWA_KV_HEADS = 2
WINDOW = 128
BLOCK = 128
ROPE_THETA = 500000.0
ROT_DIM = 16
GLA_WIDTH = 512
GLA_HEADS = 4
GLA_DK = 64
GLA_DV = 128
GLA_RANK = 16
GLA_TAU = 16.0
GLA_CHUNK = 64
EPS = 1e-5
DEPTH = 1
ALPHA = (2 * DEPTH) ** 0.25

LANES = 128
HALF = LANES // 2

QA, KA, VA, GA, QB, KB, VB, GB, RB = 0, 512, 640, 768, 1280, 1536, 1792, 2304, 2816
D_IN_PROJ = RB + GLA_RANK

TILE = 1024
CUM_ROWS = 256
OUT_ROWS = 256
W_IN_ROWS = 256
W_STEPS = RB // W_IN_ROWS
W_OUT_ROWS = 128
W_OUT_STEPS = D_MODEL // W_OUT_ROWS
assert W_OUT_STEPS <= W_STEPS
VMEM_LIMIT_BYTES = 60 * 1024 * 1024

F32 = jnp.float32
BF16 = jnp.bfloat16


def _nt(a, b):
    return lax.dot_general(a, b, (((1,), (1,)), ((), ())), preferred_element_type=F32)


def _mm(a, b):
    return jnp.dot(a, b, preferred_element_type=F32)


def _layer_kernel(tiles_per_seq, x_ref, pos_ref, w_in_blk, w_rank_t, w_up_f32, bg_ref, nw_ref,
                  w_out_blk, lng_ref, lnb_ref, invf_ref, sinks_ref,
                  o_ref, kprev_ref, vprev_ref, s_ref, mix_ref,
                  w_in_ref, w_r_ref, w_up_ref, w_out_ref, tri_ref):
    step = pl.program_id(0)

    @pl.when(step < W_STEPS)
    def _cast_weight_block():
        for k in range(W_IN_ROWS // LANES):
            w_in_ref[step * (W_IN_ROWS // LANES) + k] = (
                w_in_blk[k * LANES:(k + 1) * LANES, :].T.astype(BF16))

    @pl.when(step < W_OUT_STEPS)
    def _cast_w_out_block():
        rows = pl.ds(pl.multiple_of(step * W_OUT_ROWS, W_OUT_ROWS), W_OUT_ROWS)
        w_out_ref[rows, :] = w_out_blk[...].astype(BF16)

    @pl.when(step == 0)
    def _cast_small_weights():
        rank_rows = jnp.concatenate(
            [w_rank_t[...], jnp.zeros((LANES - GLA_RANK, D_MODEL), F32)], axis=0)
        w_r_ref[...] = rank_rows.T.astype(BF16)
        w_up_ref[...] = jnp.concatenate(
            [w_up_f32[...], jnp.zeros((LANES - GLA_RANK, GLA_HEADS * GLA_DK), F32)],
            axis=0).astype(BF16)
        ri = lax.broadcasted_iota(jnp.int32, (CUM_ROWS, CUM_ROWS), 0)
        ci = lax.broadcasted_iota(jnp.int32, (CUM_ROWS, CUM_ROWS), 1)
        tri_ref[...] = ((ri // GLA_CHUNK == ci // GLA_CHUNK) & (ci <= ri)).astype(BF16)

    t_idx = (step - W_STEPS) % tiles_per_seq

    @pl.when((step >= W_STEPS) & (t_idx == 0))
    def _reset_carries():
        kprev_ref[...] = jnp.zeros_like(kprev_ref)
        vprev_ref[...] = jnp.zeros_like(vprev_ref)
        s_ref[...] = jnp.zeros_like(s_ref)

    @pl.when(step >= W_STEPS)
    def _token_tile():
        _tile_body(t_idx, x_ref, pos_ref, bg_ref, nw_ref, lng_ref, lnb_ref, invf_ref, sinks_ref,
                   o_ref, kprev_ref, vprev_ref, s_ref, mix_ref,
                   w_in_ref, w_r_ref, w_up_ref, w_out_ref, tri_ref)


def _tile_body(t_idx, x_ref, pos_ref, bg_ref, nw_ref, lng_ref, lnb_ref, invf_ref, sinks_ref,
               o_ref, kprev_ref, vprev_ref, s_ref, mix_ref,
               w_in_ref, w_r_ref, w_up_ref, w_out_ref, tri_ref):
    xb = x_ref[...].astype(BF16)

    def proj(lo, width):
        groups = [w_in_ref[g] for g in range(lo // LANES, (lo + width) // LANES)]
        return _mm(xb, jnp.concatenate(groups, axis=1))

    lane = lax.broadcasted_iota(jnp.int32, (1, LANES), 1)
    lo_half = lane < HALF
    n_blk = TILE // BLOCK
    n_chunk = TILE // GLA_CHUNK
    n_pair = GLA_HEADS // 2
    pair_w = 2 * GLA_DV

    r_b = _mm(xb, w_r_ref[...]).astype(BF16)
    kv_new = proj(KA, 2 * LANES)
    q_a = proj(QA, SWA_WIDTH)
    logit = _mm(r_b, w_up_ref[...]) + bg_ref[...]
    log_a = jax.nn.log_sigmoid(logit) / GLA_TAU
    qk_b = proj(QB, 2 * GLA_HEADS * GLA_DK)
    q_b = qk_b[:, :GLA_HEADS * GLA_DK]
    k_b = qk_b[:, GLA_HEADS * GLA_DK:]

    tri = tri_ref[...]
    la_hi = log_a.astype(BF16)
    la_lo = (log_a - la_hi.astype(F32)).astype(BF16)
    b = jnp.concatenate(
        [_mm(tri, la_hi[r:r + CUM_ROWS]) + _mm(tri, la_lo[r:r + CUM_ROWS])
         for r in range(0, TILE, CUM_ROWS)], axis=0)
    v_b = proj(VB, GLA_WIDTH).astype(BF16)

    half = ROT_DIM // 2
    pos = pos_ref[...].astype(F32)
    ones_r = jnp.ones((HALF - ROT_DIM, BLOCK), F32)
    zeros_r = jnp.zeros((HALF - ROT_DIM, BLOCK), F32)
    zeros_h = jnp.zeros((half, BLOCK), F32)
    cos_blk, up_blk, dn_blk = [], [], []
    for i in range(TILE // BLOCK):
        ang = invf_ref[...] * pos[:, i * BLOCK:(i + 1) * BLOCK]
        c8, s8 = jnp.cos(ang), jnp.sin(ang)
        cos_blk.append(jnp.concatenate([c8, c8, ones_r] * 2, axis=0).T)
        up_blk.append(jnp.concatenate([zeros_h, s8, zeros_r] * 2, axis=0).T)
        dn_blk.append(jnp.concatenate([-s8, zeros_h, zeros_r] * 2, axis=0).T)
    cos = jnp.concatenate(cos_blk, axis=0)
    sin_up = jnp.concatenate(up_blk, axis=0)
    sin_dn = jnp.concatenate(dn_blk, axis=0)

    def rope(t):
        return (t * cos + pltpu.roll(t, ROT_DIM // 2, 1) * sin_up
                + pltpu.roll(t, LANES - ROT_DIM // 2, 1) * sin_dn)

    k_new = rope(kv_new[:, :LANES])
    v_new = kv_new[:, LANES:]
    k_cat = jnp.concatenate([kprev_ref[...], k_new], axis=0)
    v_cat = jnp.concatenate([vprev_ref[...], v_new], axis=0)
    kprev_ref[...] = k_new[TILE - BLOCK:]
    vprev_ref[...] = v_new[TILE - BLOCK:]

    k_sw = pltpu.roll(k_cat, HALF, 1)
    k_cb, k_sb = k_cat.astype(BF16), k_sw.astype(BF16)
    k_zero = jnp.zeros_like(k_cb)
    k_even = [jnp.where(lo_half, k_cb, k_zero), jnp.where(lo_half, k_sb, k_zero)]
    k_odd = [jnp.where(lo_half, k_zero, k_sb), jnp.where(lo_half, k_zero, k_cb)]
    v_t = v_cat.T.astype(BF16)

    scale = SWA_HEAD_DIM ** -0.5
    q_r = [(rope(q_a[:, cg * LANES:(cg + 1) * LANES]) * scale).astype(BF16)
           for cg in range(SWA_WIDTH // LANES)]

    swa_tiles = [(h, i) for i in range(n_blk) for h in range(SWA_KV_HEADS)]
    scores = {}
    for h, i in swa_tiles:
        rows = slice(i * BLOCK, (i + 1) * BLOCK)
        kv_rows = slice(i * BLOCK, (i + 2) * BLOCK)
        q_stack = jnp.concatenate([q_r[2 * h][rows], q_r[2 * h + 1][rows]], axis=0)
        scores[h, i] = [_nt(k_even[h][kv_rows], q_stack),
                        _nt(k_odd[h][kv_rows], q_stack)]

    kj = lax.broadcasted_iota(jnp.int32, (2 * BLOCK, BLOCK), 0)
    qj = lax.broadcasted_iota(jnp.int32, (2 * BLOCK, BLOCK), 1)
    in_window = (kj > qj) & (kj <= qj + WINDOW)
    first_blk_mask = in_window & ((kj >= BLOCK) | (t_idx > 0))
    probs, inv_den = {}, {}
    for h, i in swa_tiles:
        mask = first_blk_mask if i == 0 else in_window
        p_t, inv = [], []
        for g in range(4):
            sink = sinks_ref[4 * h + g]
            s = scores[h, i][g % 2][:, (g // 2) * BLOCK:(g // 2 + 1) * BLOCK]
            s = jnp.where(mask, s, -jnp.inf)
            m = jnp.maximum(jnp.max(s, axis=0, keepdims=True), sink)
            e = jnp.exp(s - m)
            denom = jnp.sum(e, axis=0, keepdims=True) + jnp.exp(sink - m)
            inv.append(1.0 / denom)
            p_t.append(e.astype(BF16))
        probs[h, i] = jnp.concatenate(p_t, axis=1)
        inv_den[h, i] = inv

    q_dec = (q_b * (GLA_DK ** -0.5) * jnp.exp(b)).astype(BF16)
    k_inv = (k_b * jnp.exp(-b)).astype(BF16)
    causal2 = (lax.broadcasted_iota(jnp.int32, (GLA_CHUNK, LANES), 1) & (HALF - 1)
               ) <= lax.broadcasted_iota(jnp.int32, (GLA_CHUNK, LANES), 0)
    eye = (lax.broadcasted_iota(jnp.int32, (LANES, LANES), 0)
           == lax.broadcasted_iota(jnp.int32, (LANES, LANES), 1))
    gla_tiles = [(p, c) for c in range(n_chunk) for p in range(n_pair)]
    b_end = jnp.concatenate(
        [jnp.broadcast_to(b[(c + 1) * GLA_CHUNK - 1:(c + 1) * GLA_CHUNK, :],
                          (GLA_CHUNK, GLA_HEADS * GLA_DK)) for c in range(n_chunk)], axis=0)
    k_end_t = (k_b * jnp.exp(b_end - b)).T.astype(BF16)
    v_zero = jnp.zeros((GLA_CHUNK, pair_w), BF16)
    attn, upd, decay_col = {}, {}, {}
    for p, c in gla_tiles:
        cols = slice(p * LANES, (p + 1) * LANES)
        blk = slice((c * GLA_CHUNK // BLOCK) * BLOCK, (c * GLA_CHUNK // BLOCK + 1) * BLOCK)
        v2 = v_b[c * GLA_CHUNK:(c + 1) * GLA_CHUNK, p * pair_w:(p + 1) * pair_w]
        v_pad = jnp.concatenate([v2, v_zero] if (c * GLA_CHUNK) % BLOCK == 0 else [v_zero, v2], axis=0)
        u = _mm(k_end_t[cols, blk], v_pad)
        upd[p, c] = jnp.concatenate([u[:GLA_DK, :GLA_DV], u[GLA_DK:, GLA_DV:]], axis=0)
    for p, c in gla_tiles:
        cols = slice(p * LANES, (p + 1) * LANES)
        rows = slice(c * GLA_CHUNK, (c + 1) * GLA_CHUNK)
        last = (c + 1) * GLA_CHUNK - 1
        b_last = b[last:last + 1, cols]
        ki2 = k_inv[rows, cols]
        zero = jnp.zeros_like(ki2)
        k_stack = jnp.concatenate([jnp.where(lo_half, ki2, zero),
                                   jnp.where(lo_half, zero, ki2)], axis=0)
        attn[p, c] = jnp.where(causal2, _nt(q_dec[rows, cols], k_stack), 0.0).astype(BF16)
        decay_row = jnp.broadcast_to(jnp.exp(b_last), (LANES, LANES))
        decay_col[p, c] = jnp.sum(jnp.where(eye, decay_row, 0.0), axis=1, keepdims=True)

    gate_a = jax.nn.silu(proj(GA, SWA_WIDTH))
    gate_b = jax.nn.silu(proj(GB, GLA_WIDTH))

    states = {}
    for p in range(n_pair):
        state = s_ref[p * LANES:(p + 1) * LANES, :]
        for c in range(n_chunk):
            states[p, c] = state.astype(BF16)
            state = state * decay_col[p, c] + upd[p, c]
        s_ref[p * LANES:(p + 1) * LANES, :] = state

    o_swa = {}
    for h, i in swa_tiles:
        kv_rows = slice(i * BLOCK, (i + 2) * BLOCK)
        o_swa[h, i] = _mm(v_t[h * HALF:(h + 1) * HALF, kv_rows], probs[h, i])
    zero_v = jnp.zeros((GLA_CHUNK, GLA_DV), BF16)
    o_gla = {}
    for p, c in gla_tiles:
        rows = slice(c * GLA_CHUNK, (c + 1) * GLA_CHUNK)
        v2 = v_b[rows, p * pair_w:(p + 1) * pair_w]
        st = states[p, c]
        rhs = jnp.concatenate([
            jnp.concatenate([v2[:, :GLA_DV], zero_v], axis=1),
            jnp.concatenate([zero_v, v2[:, GLA_DV:]], axis=1),
            jnp.concatenate([st[:GLA_DK], zero_v], axis=1),
            jnp.concatenate([zero_v, st[GLA_DK:]], axis=1)], axis=0)
        lhs = jnp.concatenate([attn[p, c], q_dec[rows, p * LANES:(p + 1) * LANES]], axis=1)
        o_gla[p, c] = _mm(lhs, rhs)

    for h, i in swa_tiles:
        rows = slice(i * BLOCK, (i + 1) * BLOCK)
        o_t, inv = o_swa[h, i], inv_den[h, i]
        for c in range(2):
            cg = 2 * h + c
            o2_t = jnp.concatenate(
                [o_t[:, (2 * c + par) * BLOCK:(2 * c + par + 1) * BLOCK] * inv[2 * c + par]
                 for par in range(2)], axis=0)
            cg_cols = slice(cg * LANES, (cg + 1) * LANES)
            out = o2_t.T * gate_a[rows, cg_cols]
            mix_ref[rows, cg_cols] = out.astype(BF16)
    norm_w = nw_ref[...]
    for p, c in gla_tiles:
        rows = slice(c * GLA_CHUNK, (c + 1) * GLA_CHUNK)
        for par in range(2):
            o = o_gla[p, c][:, par * GLA_DV:(par + 1) * GLA_DV]
            o = o * lax.rsqrt(jnp.mean(jnp.square(o), axis=-1, keepdims=True) + EPS) * norm_w
            c0 = (2 * p + par) * GLA_DV
            out = o * gate_b[rows, c0:c0 + GLA_DV]
            mix_ref[rows, SWA_WIDTH + c0:SWA_WIDTH + c0 + GLA_DV] = out.astype(BF16)

    for r in range(0, TILE, OUT_ROWS):
        rows = slice(r, r + OUT_ROWS)
        y = _mm(mix_ref[rows, :], w_out_ref[...]) + ALPHA * x_ref[rows, :]
        mu = jnp.mean(y, axis=-1, keepdims=True)
        yc = y - mu
        var = jnp.mean(jnp.square(yc), axis=-1, keepdims=True)
        o_ref[rows, :] = yc * lax.rsqrt(var + EPS) * lng_ref[...] + lnb_ref[...]


def _rope_freq_rows():
    half = ROT_DIM // 2
    inv_freq = ROPE_THETA ** (-jnp.arange(half, dtype=F32) / half)
    return jnp.broadcast_to(inv_freq[:, None], (half, BLOCK))


@jax.jit
def kernel(x, positions, w_in, gla_w_gate_up, gla_b_gate, attn_sinks, gla_norm_w, w_out, ln_g, ln_b):
    B, S, D = x.shape
    assert D == D_MODEL and S % TILE == 0 and w_in.shape == (DEPTH, D_MODEL, D_IN_PROJ)
    tiles_per_seq = S // TILE
    n_tiles = B * tiles_per_seq
    x_tiles = x.reshape(n_tiles, TILE, D_MODEL)
    pos_rows = positions.reshape(n_tiles, 1, TILE)
    w_in_t = jnp.swapaxes(w_in, 1, 2)

    def const(shape):
        return pl.BlockSpec((None,) + shape, lambda s: (0,) * (len(shape) + 1),
                            pipeline_mode=pl.Buffered(1))

    tile = lambda s: (jnp.maximum(s - W_STEPS, 0), 0, 0)
    grid_spec = pl.GridSpec(
        grid=(W_STEPS + n_tiles,),
        in_specs=[
            pl.BlockSpec((None, TILE, D_MODEL), tile),
            pl.BlockSpec((None, 1, TILE), tile),
            pl.BlockSpec((None, W_IN_ROWS, D_MODEL), lambda s: (0, jnp.minimum(s, W_STEPS - 1), 0)),
            const((GLA_RANK, D_MODEL)),
            const((GLA_RANK, GLA_HEADS * GLA_DK)),
            const((1, GLA_HEADS * GLA_DK)),
            const((1, GLA_DV)),
            pl.BlockSpec((None, W_OUT_ROWS, D_MODEL), lambda s: (0, jnp.minimum(s, W_OUT_STEPS - 1), 0)),
            const((1, D_MODEL)),
            const((1, D_MODEL)),
            pl.BlockSpec((ROT_DIM // 2, BLOCK), lambda s: (0, 0)),
            pl.BlockSpec(memory_space=pltpu.SMEM),
        ],
        out_specs=pl.BlockSpec((None, TILE, D_MODEL), tile),
        scratch_shapes=[
            pltpu.VMEM((BLOCK, LANES), F32),
            pltpu.VMEM((BLOCK, LANES), F32),
            pltpu.VMEM((GLA_HEADS * GLA_DK, GLA_DV), F32),
            pltpu.VMEM((TILE, D_MODEL), BF16),
            pltpu.VMEM((RB // LANES, D_MODEL, LANES), BF16),
            pltpu.VMEM((D_MODEL, LANES), BF16),
            pltpu.VMEM((LANES, GLA_HEADS * GLA_DK), BF16),
            pltpu.VMEM((D_MODEL, D_MODEL), BF16),
            pltpu.VMEM((CUM_ROWS, CUM_ROWS), BF16),
        ],
    )
    row = lambda a: a.reshape(DEPTH, 1, -1)
    out = pl.pallas_call(
        functools.partial(_layer_kernel, tiles_per_seq),
        grid_spec=grid_spec,
        out_shape=jax.ShapeDtypeStruct((n_tiles, TILE, D_MODEL), x.dtype),
        compiler_params=pltpu.CompilerParams(
            dimension_semantics=("arbitrary",),
            vmem_limit_bytes=VMEM_LIMIT_BYTES),
        name="hymba_layer",
    )(x_tiles, pos_rows, w_in_t, w_in_t[:, RB:, :], gla_w_gate_up, row(gla_b_gate),
      row(gla_norm_w), w_out, row(ln_g), row(ln_b), _rope_freq_rows(), attn_sinks[0])
    return out.reshape(B, S, D_MODEL)
```

```python
import jax
import jax.numpy as jnp
from jax import lax
from jax.experimental import pallas as pl
from jax.experimental.pallas import tpu as pltpu

D_MODEL = 1024
SWA_WIDTH = 512
SWA_HEAD_DIM = 64
SWA_Q_HEADS = 8
SWA_KV_HEADS = 2
WINDOW = 128
BLOCK = 128
ROPE_THETA = 500000.0
ROT_DIM = 16
GLA_WIDTH = 512
GLA_HEADS = 4
GLA_DK = 64
GLA_DV = 128
GLA_RANK = 16
GLA_TAU = 16.0
GLA_CHUNK = 64
EPS = 1e-5
DEPTH = 1
ALPHA = (2 * DEPTH) ** 0.25

LANES = 128
HALF = LANES // 2

QA, KA, VA, GA, QB, KB, VB, GB, RB = 0, 512, 640, 768, 1280, 1536, 1792, 2304, 2816
D_IN_PROJ = RB + GLA_RANK

TILE = 512
CUM_ROWS = 256
OUT_ROWS = 256
GLA_BLOCK = 2 * GLA_CHUNK
W_CAST_ROWS = 128
VMEM_LIMIT_BYTES = 60 * 1024 * 1024

F32 = jnp.float32
BF16 = jnp.bfloat16


def _nt(a, b):
    return lax.dot_general(a, b, (((1,), (1,)), ((), ())), preferred_element_type=F32)


def _mm(a, b):
    return jnp.dot(a, b, preferred_element_type=F32)


def _layer_kernel(x_ref, pos_ref, w_in_t_f32, w_up_f32, bg_ref, nw_ref, w_out_f32,
                  lng_ref, lnb_ref, invf_ref, sinks_ref,
                  o_ref, kprev_ref, vprev_ref, s_ref, mix_ref,
                  w_in_ref, w_r_ref, w_up_ref, w_out_ref, tri_ref):
    t_idx = pl.program_id(1)

    @pl.when((pl.program_id(0) == 0) & (t_idx == 0))
    def _cast_weights():
        def cast_group(g, carry):
            rows = pl.ds(pl.multiple_of(g * LANES, LANES), LANES)
            w_in_ref[g] = w_in_t_f32[rows, :].T.astype(BF16)
            return carry
        lax.fori_loop(0, RB // LANES, cast_group, 0)
        rank_rows = jnp.concatenate(
            [w_in_t_f32[RB:, :], jnp.zeros((LANES - GLA_RANK, D_MODEL), F32)], axis=0)
        w_r_ref[...] = rank_rows.T.astype(BF16)
        ri = lax.broadcasted_iota(jnp.int32, (CUM_ROWS, CUM_ROWS), 0)
        ci = lax.broadcasted_iota(jnp.int32, (CUM_ROWS, CUM_ROWS), 1)
        tri_ref[...] = ((ri // GLA_CHUNK == ci // GLA_CHUNK) & (ci <= ri)).astype(BF16)

        def cast_rows(r, carry):
            rows = pl.ds(pl.multiple_of(r * W_CAST_ROWS, W_CAST_ROWS), W_CAST_ROWS)
            w_out_ref[rows, :] = w_out_f32[rows, :].astype(BF16)
            return carry
        lax.fori_loop(0, D_MODEL // W_CAST_ROWS, cast_rows, 0)
        w_up_ref[...] = jnp.concatenate(
            [w_up_f32[...], jnp.zeros((LANES - GLA_RANK, GLA_HEADS * GLA_DK), F32)],
            axis=0).astype(BF16)

    @pl.when(t_idx == 0)
    def _reset_carries():
        kprev_ref[...] = jnp.zeros_like(kprev_ref)
        vprev_ref[...] = jnp.zeros_like(vprev_ref)
        s_ref[...] = jnp.zeros_like(s_ref)

    xb = x_ref[...].astype(BF16)

    def proj(lo, width):
        groups = [w_in_ref[g] for g in range(lo // LANES, (lo + width) // LANES)]
        return _mm(xb, jnp.concatenate(groups, axis=1))

    lane = lax.broadcasted_iota(jnp.int32, (1, LANES), 1)
    lo_half = lane < HALF
    n_blk = TILE // BLOCK
    n_gblk = TILE // GLA_BLOCK
    n_pair = GLA_HEADS // 2
    pair_w = 2 * GLA_DV
    gla_lanes = GLA_HEADS * GLA_DK

    r_b = _mm(xb, w_r_ref[...]).astype(BF16)
    kv_new = proj(KA, 2 * LANES)
    q_a = proj(QA, SWA_WIDTH)
    logit = _mm(r_b, w_up_ref[...]) + bg_ref[...]
    log_a = jax.nn.log_sigmoid(logit) / GLA_TAU
    qk_b = proj(QB, 2 * gla_lanes)
    q_b = qk_b[:, :gla_lanes]
    k_b = qk_b[:, gla_lanes:]

    tri = tri_ref[...]
    la_hi = log_a.astype(BF16)
    la_lo = (log_a - la_hi.astype(F32)).astype(BF16)
    b = jnp.concatenate(
        [_mm(tri, la_hi[r:r + CUM_ROWS]) + _mm(tri, la_lo[r:r + CUM_ROWS])
         for r in range(0, TILE, CUM_ROWS)], axis=0)
    v_b = proj(VB, GLA_WIDTH).astype(BF16)

    half = ROT_DIM // 2
    pos = pos_ref[...].astype(F32)
    ones_r = jnp.ones((HALF - ROT_DIM, BLOCK), F32)
    zeros_r = jnp.zeros((HALF - ROT_DIM, BLOCK), F32)
    zeros_h = jnp.zeros((half, BLOCK), F32)
    cos_blk, up_blk, dn_blk = [], [], []
    for i in range(TILE // BLOCK):
        ang = invf_ref[...] * pos[:, i * BLOCK:(i + 1) * BLOCK]
        c8, s8 = jnp.cos(ang), jnp.sin(ang)
        cos_blk.append(jnp.concatenate([c8, c8, ones_r] * 2, axis=0).T)
        up_blk.append(jnp.concatenate([zeros_h, s8, zeros_r] * 2, axis=0).T)
        dn_blk.append(jnp.concatenate([-s8, zeros_h, zeros_r] * 2, axis=0).T)
    cos = jnp.concatenate(cos_blk, axis=0)
    sin_up = jnp.concatenate(up_blk, axis=0)
    sin_dn = jnp.concatenate(dn_blk, axis=0)

    def rope(t):
        return (t * cos + pltpu.roll(t, ROT_DIM // 2, 1) * sin_up
                + pltpu.roll(t, LANES - ROT_DIM // 2, 1) * sin_dn)

    k_new = rope(kv_new[:, :LANES])
    v_new = kv_new[:, LANES:]
    k_cat = jnp.concatenate([kprev_ref[...], k_new], axis=0)
    v_cat = jnp.concatenate([vprev_ref[...], v_new], axis=0)
    kprev_ref[...] = k_new[TILE - BLOCK:]
    vprev_ref[...] = v_new[TILE - BLOCK:]

    k_sw = pltpu.roll(k_cat, HALF, 1)
    k_cb, k_sb = k_cat.astype(BF16), k_sw.astype(BF16)
    k_zero = jnp.zeros_like(k_cb)
    k_even = [jnp.where(lo_half, k_cb, k_zero), jnp.where(lo_half, k_sb, k_zero)]
    k_odd = [jnp.where(lo_half, k_zero, k_sb), jnp.where(lo_half, k_zero, k_cb)]
    v_t = v_cat.T.astype(BF16)

    scale = SWA_HEAD_DIM ** -0.5
    q_t = [(rope(q_a[:, cg * LANES:(cg + 1) * LANES]) * scale).T.astype(BF16)
           for cg in range(SWA_WIDTH // LANES)]

    swa_tiles = [(h, i) for i in range(n_blk) for h in range(SWA_KV_HEADS)]
    scores = {}
    for h, i in swa_tiles:
        rows = slice(i * BLOCK, (i + 1) * BLOCK)
        kv_rows = slice(i * BLOCK, (i + 2) * BLOCK)
        q_pair = jnp.concatenate([q_t[2 * h][:, rows], q_t[2 * h + 1][:, rows]], axis=1)
        scores[h, i] = [_mm(k_even[h][kv_rows], q_pair),
                        _mm(k_odd[h][kv_rows], q_pair)]

    kj = lax.broadcasted_iota(jnp.int32, (2 * BLOCK, BLOCK), 0)
    qj = lax.broadcasted_iota(jnp.int32, (2 * BLOCK, BLOCK), 1)
    in_window = (kj > qj) & (kj <= qj + WINDOW)
    first_blk_mask = in_window & ((kj >= BLOCK) | (t_idx > 0))
    probs, inv_den = {}, {}
    for h, i in swa_tiles:
        mask = first_blk_mask if i == 0 else in_window
        p_t, inv = [], []
        for g in range(4):
            sink = sinks_ref[4 * h + g]
            s = scores[h, i][g % 2][:, (g // 2) * BLOCK:(g // 2 + 1) * BLOCK]
            s = jnp.where(mask, s, -jnp.inf)
            m = jnp.maximum(jnp.max(s, axis=0, keepdims=True), sink)
            e = jnp.exp(s - m)
            denom = jnp.sum(e, axis=0, keepdims=True) + jnp.exp(sink - m)
            inv.append(1.0 / denom)
            p_t.append(e.astype(BF16))
        probs[h, i] = jnp.concatenate(p_t, axis=1)
        inv_den[h, i] = inv

    def chunk_end_rows(c):
        return jnp.broadcast_to(b[(c + 1) * GLA_CHUNK - 1:(c + 1) * GLA_CHUNK, :], (GLA_CHUNK, gla_lanes))

    zeros_c = jnp.zeros((GLA_CHUNK, gla_lanes), F32)
    l1 = jnp.concatenate([chunk_end_rows(2 * g) for g in range(n_gblk) for _ in range(2)], axis=0)
    l1_c1 = jnp.concatenate([blk for g in range(n_gblk) for blk in (chunk_end_rows(2 * g), zeros_c)], axis=0)
    l1_c2 = l1 - l1_c1
    l2 = jnp.concatenate([chunk_end_rows(2 * g + 1) for g in range(n_gblk) for _ in range(2)], axis=0)
    pivot = b - l1_c1
    q_s = q_b * (GLA_DK ** -0.5)
    q_att = (q_s * jnp.exp(pivot)).astype(BF16)
    k_att_t = (k_b * jnp.exp(-pivot)).T.astype(BF16)
    q_dec = (q_s * jnp.exp(b + l1_c2)).astype(BF16)
    k_end_t = (k_b * jnp.exp(l1_c1 - b + l2)).T.astype(BF16)

    causal_blk = (lax.broadcasted_iota(jnp.int32, (GLA_BLOCK, 2 * GLA_BLOCK), 1) & (GLA_BLOCK - 1)
                  ) <= lax.broadcasted_iota(jnp.int32, (GLA_BLOCK, 2 * GLA_BLOCK), 0)
    eye = (lax.broadcasted_iota(jnp.int32, (LANES, LANES), 0)
           == lax.broadcasted_iota(jnp.int32, (LANES, LANES), 1))
    gla_tiles = [(p, g) for g in range(n_gblk) for p in range(n_pair)]
    zero_kt = jnp.zeros((GLA_DK, GLA_BLOCK), BF16)
    attn, upd, decay_col = {}, {}, {}
    for p, g in gla_tiles:
        cols = slice(p * LANES, (p + 1) * LANES)
        blk = slice(g * GLA_BLOCK, (g + 1) * GLA_BLOCK)
        u = _mm(k_end_t[cols, blk], v_b[blk, p * pair_w:(p + 1) * pair_w])
        upd[p, g] = jnp.concatenate([u[:GLA_DK, :GLA_DV], u[GLA_DK:, GLA_DV:]], axis=0)
    for p, g in gla_tiles:
        cols = slice(p * LANES, (p + 1) * LANES)
        blk = slice(g * GLA_BLOCK, (g + 1) * GLA_BLOCK)
        ka2 = k_att_t[cols, blk]
        k_diag = jnp.concatenate([jnp.concatenate([ka2[:GLA_DK], zero_kt], axis=1),
                                  jnp.concatenate([zero_kt, ka2[GLA_DK:]], axis=1)], axis=0)
        attn[p, g] = jnp.where(causal_blk, _mm(q_att[blk, cols], k_diag), 0.0).astype(BF16)
        last = (g + 1) * GLA_BLOCK - 1
        total = l1[last:last + 1, cols] + l2[last:last + 1, cols]
        decay_row = jnp.broadcast_to(jnp.exp(total), (LANES, LANES))
        decay_col[p, g] = jnp.sum(jnp.where(eye, decay_row, 0.0), axis=1, keepdims=True)

    gate_a = jax.nn.silu(proj(GA, SWA_WIDTH))
    gate_b = jax.nn.silu(proj(GB, GLA_WIDTH))

    states = {}
    for p in range(n_pair):
        state = s_ref[p * LANES:(p + 1) * LANES, :]
        for g in range(n_gblk):
            states[p, g] = state.astype(BF16)
            state = state * decay_col[p, g] + upd[p, g]
        s_ref[p * LANES:(p + 1) * LANES, :] = state

    o_swa = {}
    for h, i in swa_tiles:
        kv_rows = slice(i * BLOCK, (i + 2) * BLOCK)
        o_swa[h, i] = _mm(v_t[h * HALF:(h + 1) * HALF, kv_rows], probs[h, i])
    zero_vb = jnp.zeros((GLA_BLOCK, GLA_DV), BF16)
    zero_vs = jnp.zeros((GLA_DK, GLA_DV), BF16)
    o_gla = {}
    for p, g in gla_tiles:
        blk = slice(g * GLA_BLOCK, (g + 1) * GLA_BLOCK)
        v2 = v_b[blk, p * pair_w:(p + 1) * pair_w]
        st = states[p, g]
        rhs = jnp.concatenate([
            jnp.concatenate([v2[:, :GLA_DV], zero_vb], axis=1),
            jnp.concatenate([zero_vb, v2[:, GLA_DV:]], axis=1),
            jnp.concatenate([st[:GLA_DK], zero_vs], axis=1),
            jnp.concatenate([zero_vs, st[GLA_DK:]], axis=1)], axis=0)
        lhs = jnp.concatenate([attn[p, g], q_dec[blk, p * LANES:(p + 1) * LANES]], axis=1)
        o_gla[p, g] = _mm(lhs, rhs)

    for h, i in swa_tiles:
        rows = slice(i * BLOCK, (i + 1) * BLOCK)
        o_t, inv = o_swa[h, i], inv_den[h, i]
        for c in range(2):
            cg = 2 * h + c
            o2_t = jnp.concatenate(
                [o_t[:, (2 * c + par) * BLOCK:(2 * c + par + 1) * BLOCK] * inv[2 * c + par]
                 for par in range(2)], axis=0)
            cg_cols = slice(cg * LANES, (cg + 1) * LANES)
            out = o2_t.T * gate_a[rows, cg_cols]
            mix_ref[rows, cg_cols] = out.astype(BF16)
    norm_w = nw_ref[...]
    for p, g in gla_tiles:
        rows = slice(g * GLA_BLOCK, (g + 1) * GLA_BLOCK)
        for par in range(2):
            o = o_gla[p, g][:, par * GLA_DV:(par + 1) * GLA_DV]
            o = o * lax.rsqrt(jnp.mean(jnp.square(o), axis=-1, keepdims=True) + EPS) * norm_w
            c0 = (2 * p + par) * GLA_DV
            out = o * gate_b[rows, c0:c0 + GLA_DV]
            mix_ref[rows, SWA_WIDTH + c0:SWA_WIDTH + c0 + GLA_DV] = out.astype(BF16)

    for r in range(0, TILE, OUT_ROWS):
        rows = slice(r, r + OUT_ROWS)
        y = _mm(mix_ref[rows, :], w_out_ref[...]) + ALPHA * x_ref[rows, :]
        mu = jnp.mean(y, axis=-1, keepdims=True)
        yc = y - mu
        var = jnp.mean(jnp.square(yc), axis=-1, keepdims=True)
        o_ref[rows, :] = yc * lax.rsqrt(var + EPS) * lng_ref[...] + lnb_ref[...]


def _rope_freq_rows():
    half = ROT_DIM // 2
    inv_freq = ROPE_THETA ** (-jnp.arange(half, dtype=F32) / half)
    return jnp.broadcast_to(inv_freq[:, None], (half, BLOCK))


@jax.jit
def kernel(x, positions, w_in, gla_w_gate_up, gla_b_gate, attn_sinks, gla_norm_w, w_out, ln_g, ln_b):
    B, S, D = x.shape
    assert D == D_MODEL and S % TILE == 0 and w_in.shape == (DEPTH, D_MODEL, D_IN_PROJ)
    n_tiles = S // TILE
    pos_rows = positions.reshape(B * n_tiles, 1, TILE)

    def const(shape):
        return pl.BlockSpec((None,) + shape, lambda b, t: (0,) * (len(shape) + 1),
                            pipeline_mode=pl.Buffered(1))

    grid_spec = pl.GridSpec(
        grid=(B, n_tiles),
        in_specs=[
            pl.BlockSpec((None, TILE, D_MODEL), lambda b, t: (b, t, 0)),
            pl.BlockSpec((None, 1, TILE), lambda b, t: (b * n_tiles + t, 0, 0)),
            const((D_IN_PROJ, D_MODEL)),
            const((GLA_RANK, GLA_HEADS * GLA_DK)),
            const((1, GLA_HEADS * GLA_DK)),
            const((1, GLA_DV)),
            const((D_MODEL, D_MODEL)),
            const((1, D_MODEL)),
            const((1, D_MODEL)),
            pl.BlockSpec((ROT_DIM // 2, BLOCK), lambda b, t: (0, 0)),
            pl.BlockSpec(memory_space=pltpu.SMEM),
        ],
        out_specs=pl.BlockSpec((None, TILE, D_MODEL), lambda b, t: (b, t, 0)),
        scratch_shapes=[
            pltpu.VMEM((BLOCK, LANES), F32),
            pltpu.VMEM((BLOCK, LANES), F32),
            pltpu.VMEM((GLA_HEADS * GLA_DK, GLA_DV), F32),
            pltpu.VMEM((TILE, D_MODEL), BF16),
            pltpu.VMEM((RB // LANES, D_MODEL, LANES), BF16),
            pltpu.VMEM((D_MODEL, LANES), BF16),
            pltpu.VMEM((LANES, GLA_HEADS * GLA_DK), BF16),
            pltpu.VMEM((D_MODEL, D_MODEL), BF16),
            pltpu.VMEM((CUM_ROWS, CUM_ROWS), BF16),
        ],
    )
    row = lambda a: a.reshape(DEPTH, 1, -1)
    return pl.pallas_call(
        _layer_kernel,
        grid_spec=grid_spec,
        out_shape=jax.ShapeDtypeStruct((B, S, D_MODEL), x.dtype),
        compiler_params=pltpu.CompilerParams(
            dimension_semantics=("arbitrary", "arbitrary"),
            vmem_limit_bytes=VMEM_LIMIT_BYTES),
        name="hymba_layer",
    )(x, pos_rows, jnp.swapaxes(w_in, 1, 2), gla_w_gate_up, row(gla_b_gate), row(gla_norm_w),
      w_out, row(ln_g), row(ln_b), _rope_freq_rows(), attn_sinks[0])
```

```python
import jax
import jax.numpy as jnp
from jax import lax
from jax.experimental import pallas as pl
from jax.experimental.pallas import tpu as pltpu

D_MODEL = 1024
SWA_WIDTH = 512
SWA_HEAD_DIM = 64
SWA_Q_HEADS = 8
SWA_KV_HEADS = 2
WINDOW = 128
BLOCK = 128
ROPE_THETA = 500000.0
ROT_DIM = 16
GLA_WIDTH = 512
GLA_HEADS = 4
GLA_DK = 64
GLA_DV = 128
GLA_RANK = 16
GLA_TAU = 16.0
GLA_CHUNK = 64
EPS = 1e-5
DEPTH = 1
ALPHA = (2 * DEPTH) ** 0.25

LANES = 128
HALF = LANES // 2

QA, KA, VA, GA, QB, KB, VB, GB, RB = 0, 512, 640, 768, 1280, 1536, 1792, 2304, 2816
D_IN_PROJ = RB + GLA_RANK

TILE = 512
CUM_ROWS = 256
OUT_ROWS = 256
GLA_BLOCK = 2 * GLA_CHUNK
W_CAST_ROWS = 128
VMEM_LIMIT_BYTES = 60 * 1024 * 1024

F32 = jnp.float32
BF16 = jnp.bfloat16


def _nt(a, b):
    return lax.dot_general(a, b, (((1,), (1,)), ((), ())), preferred_element_type=F32)


def _mm(a, b):
    return jnp.dot(a, b, preferred_element_type=F32)


def _layer_kernel(x_ref, pos_ref, w_in_t_f32, w_up_f32, bg_ref, nw_ref, w_out_f32,
                  lng_ref, lnb_ref, invf_ref, sinks_ref,
                  o_ref, kprev_ref, vprev_ref, s_ref, mix_ref,
                  w_in_ref, w_r_ref, w_up_ref, w_out_ref, tri_ref):
    t_idx = pl.program_id(1)

    @pl.when((pl.program_id(0) == 0) & (t_idx == 0))
    def _cast_weights():
        def cast_group(g, carry):
            rows = pl.ds(pl.multiple_of(g * LANES, LANES), LANES)
            w_in_ref[g] = w_in_t_f32[rows, :].T.astype(BF16)
            return carry
        lax.fori_loop(0, RB // LANES, cast_group, 0)
        rank_rows = jnp.concatenate(
            [w_in_t_f32[RB:, :], jnp.zeros((LANES - GLA_RANK, D_MODEL), F32)], axis=0)
        w_r_ref[...] = rank_rows.T.astype(BF16)
        ri = lax.broadcasted_iota(jnp.int32, (CUM_ROWS, CUM_ROWS), 0)
        ci = lax.broadcasted_iota(jnp.int32, (CUM_ROWS, CUM_ROWS), 1)
        tri_ref[...] = ((ri // GLA_CHUNK == ci // GLA_CHUNK) & (ci <= ri)).astype(BF16)

        def cast_rows(r, carry):
            rows = pl.ds(pl.multiple_of(r * W_CAST_ROWS, W_CAST_ROWS), W_CAST_ROWS)
            w_out_ref[rows, :] = w_out_f32[rows, :].astype(BF16)
            return carry
        lax.fori_loop(0, D_MODEL // W_CAST_ROWS, cast_rows, 0)
        w_up_ref[...] = jnp.concatenate(
            [w_up_f32[...], jnp.zeros((LANES - GLA_RANK, GLA_HEADS * GLA_DK), F32)],
            axis=0).astype(BF16)

    @pl.when(t_idx == 0)
    def _reset_carries():
        kprev_ref[...] = jnp.zeros_like(kprev_ref)
        vprev_ref[...] = jnp.zeros_like(vprev_ref)
        s_ref[...] = jnp.zeros_like(s_ref)

    xb = x_ref[...].astype(BF16)

    def proj(lo, width):
        groups = [w_in_ref[g] for g in range(lo // LANES, (lo + width) // LANES)]
        return _mm(xb, jnp.concatenate(groups, axis=1))

    lane = lax.broadcasted_iota(jnp.int32, (1, LANES), 1)
    lo_half = lane < HALF
    n_blk = TILE // BLOCK
    n_gblk = TILE // GLA_BLOCK
    n_pair = GLA_HEADS // 2
    pair_w = 2 * GLA_DV
    gla_lanes = GLA_HEADS * GLA_DK

    r_b = _mm(xb, w_r_ref[...]).astype(BF16)
    kv_new = proj(KA, 2 * LANES)
    q_a = proj(QA, SWA_WIDTH)
    logit = _mm(r_b, w_up_ref[...]) + bg_ref[...]
    log_a = jax.nn.log_sigmoid(logit) / GLA_TAU
    qk_b = proj(QB, 2 * gla_lanes)
    q_b = qk_b[:, :gla_lanes]
    k_b = qk_b[:, gla_lanes:]

    tri = tri_ref[...]
    la_hi = log_a.astype(BF16)
    la_lo = (log_a - la_hi.astype(F32)).astype(BF16)
    b = jnp.concatenate(
        [_mm(tri, la_hi[r:r + CUM_ROWS]) + _mm(tri, la_lo[r:r + CUM_ROWS])
         for r in range(0, TILE, CUM_ROWS)], axis=0)

    half = ROT_DIM // 2
    pos = pos_ref[...].astype(F32)
    ones_r = jnp.ones((HALF - ROT_DIM, BLOCK), F32)
    zeros_r = jnp.zeros((HALF - ROT_DIM, BLOCK), F32)
    zeros_h = jnp.zeros((half, BLOCK), F32)
    cos_blk, up_blk, dn_blk = [], [], []
    for i in range(TILE // BLOCK):
        ang = invf_ref[...] * pos[:, i * BLOCK:(i + 1) * BLOCK]
        c8, s8 = jnp.cos(ang), jnp.sin(ang)
        cos_blk.append(jnp.concatenate([c8, c8, ones_r] * 2, axis=0).T)
        up_blk.append(jnp.concatenate([zeros_h, s8, zeros_r] * 2, axis=0).T)
        dn_blk.append(jnp.concatenate([-s8, zeros_h, zeros_r] * 2, axis=0).T)
    cos = jnp.concatenate(cos_blk, axis=0)
    sin_up = jnp.concatenate(up_blk, axis=0)
    sin_dn = jnp.concatenate(dn_blk, axis=0)

    def rope(t):
        return (t * cos + pltpu.roll(t, ROT_DIM // 2, 1) * sin_up
                + pltpu.roll(t, LANES - ROT_DIM // 2, 1) * sin_dn)

    k_new = rope(kv_new[:, :LANES])
    v_new = kv_new[:, LANES:]
    k_cat = jnp.concatenate([kprev_ref[...], k_new], axis=0)
    v_cat = jnp.concatenate([vprev_ref[...], v_new], axis=0)
    kprev_ref[...] = k_new[TILE - BLOCK:]
    vprev_ref[...] = v_new[TILE - BLOCK:]

    k_sw = pltpu.roll(k_cat, HALF, 1)
    k_cb, k_sb = k_cat.astype(BF16), k_sw.astype(BF16)
    k_zero = jnp.zeros_like(k_cb)
    k_even = [jnp.where(lo_half, k_cb, k_zero), jnp.where(lo_half, k_sb, k_zero)]
    k_odd = [jnp.where(lo_half, k_zero, k_sb), jnp.where(lo_half, k_zero, k_cb)]
    v_t = v_cat.T.astype(BF16)

    scale = SWA_HEAD_DIM ** -0.5
    q_t = [(rope(q_a[:, cg * LANES:(cg + 1) * LANES]) * scale).T.astype(BF16)
           for cg in range(SWA_WIDTH // LANES)]

    swa_tiles = [(h, i) for i in range(n_blk) for h in range(SWA_KV_HEADS)]
    scores = {}
    gate_b_parts = [proj(GB, 2 * LANES)]
    v_parts = []
    for n, (h, i) in enumerate(swa_tiles):
        if n % (len(swa_tiles) // 2) == 0:
            v_parts.append(proj(VB + len(v_parts) * 2 * LANES, 2 * LANES))
        rows = slice(i * BLOCK, (i + 1) * BLOCK)
        kv_rows = slice(i * BLOCK, (i + 2) * BLOCK)
        q_pair = jnp.concatenate([q_t[2 * h][:, rows], q_t[2 * h + 1][:, rows]], axis=1)
        scores[h, i] = [_mm(k_even[h][kv_rows], q_pair),
                        _mm(k_odd[h][kv_rows], q_pair)]
    v_b = jnp.concatenate(v_parts, axis=1).astype(BF16)

    kj = lax.broadcasted_iota(jnp.int32, (2 * BLOCK, BLOCK), 0)
    qj = lax.broadcasted_iota(jnp.int32, (2 * BLOCK, BLOCK), 1)
    in_window = (kj > qj) & (kj <= qj + WINDOW)
    first_blk_mask = in_window & ((kj >= BLOCK) | (t_idx > 0))
    probs, inv_den = {}, {}
    for h, i in swa_tiles:
        mask = first_blk_mask if i == 0 else in_window
        p_t, inv = [], []
        for g in range(4):
            sink = sinks_ref[4 * h + g]
            s = scores[h, i][g % 2][:, (g // 2) * BLOCK:(g // 2 + 1) * BLOCK]
            s = jnp.where(mask, s, -jnp.inf)
            m = jnp.maximum(jnp.max(s, axis=0, keepdims=True), sink)
            e = jnp.exp(s - m)
            denom = jnp.sum(e, axis=0, keepdims=True) + jnp.exp(sink - m)
            inv.append(1.0 / denom)
            p_t.append(e.astype(BF16))
        probs[h, i] = jnp.concatenate(p_t, axis=1)
        inv_den[h, i] = inv

    def chunk_end_rows(c):
        return jnp.broadcast_to(b[(c + 1) * GLA_CHUNK - 1:(c + 1) * GLA_CHUNK, :], (GLA_CHUNK, gla_lanes))

    zeros_c = jnp.zeros((GLA_CHUNK, gla_lanes), F32)
    l1 = jnp.concatenate([chunk_end_rows(2 * g) for g in range(n_gblk) for _ in range(2)], axis=0)
    l1_c1 = jnp.concatenate([blk for g in range(n_gblk) for blk in (chunk_end_rows(2 * g), zeros_c)], axis=0)
    l1_c2 = l1 - l1_c1
    l2 = jnp.concatenate([chunk_end_rows(2 * g + 1) for g in range(n_gblk) for _ in range(2)], axis=0)
    pivot = b - l1_c1
    q_s = q_b * (GLA_DK ** -0.5)
    q_att = (q_s * jnp.exp(pivot)).astype(BF16)
    k_att_t = (k_b * jnp.exp(-pivot)).T.astype(BF16)
    q_dec = (q_s * jnp.exp(b + l1_c2)).astype(BF16)
    k_end_t = (k_b * jnp.exp(l1_c1 - b + l2)).T.astype(BF16)

    causal_blk = (lax.broadcasted_iota(jnp.int32, (GLA_BLOCK, 2 * GLA_BLOCK), 1) & (GLA_BLOCK - 1)
                  ) <= lax.broadcasted_iota(jnp.int32, (GLA_BLOCK, 2 * GLA_BLOCK), 0)
    eye = (lax.broadcasted_iota(jnp.int32, (LANES, LANES), 0)
           == lax.broadcasted_iota(jnp.int32, (LANES, LANES), 1))
    gla_tiles = [(p, g) for g in range(n_gblk) for p in range(n_pair)]
    zero_kt = jnp.zeros((GLA_DK, GLA_BLOCK), BF16)
    attn, upd, decay_col = {}, {}, {}
    gate_parts = [proj(GA, 2 * LANES)]
    for p, g in gla_tiles:
        cols = slice(p * LANES, (p + 1) * LANES)
        blk = slice(g * GLA_BLOCK, (g + 1) * GLA_BLOCK)
        u = _mm(k_end_t[cols, blk], v_b[blk, p * pair_w:(p + 1) * pair_w])
        upd[p, g] = jnp.concatenate([u[:GLA_DK, :GLA_DV], u[GLA_DK:, GLA_DV:]], axis=0)
    gate_parts.append(proj(GA + 2 * LANES, 2 * LANES))
    for p, g in gla_tiles:
        cols = slice(p * LANES, (p + 1) * LANES)
        blk = slice(g * GLA_BLOCK, (g + 1) * GLA_BLOCK)
        ka2 = k_att_t[cols, blk]
        k_diag = jnp.concatenate([jnp.concatenate([ka2[:GLA_DK], zero_kt], axis=1),
                                  jnp.concatenate([zero_kt, ka2[GLA_DK:]], axis=1)], axis=0)
        attn[p, g] = jnp.where(causal_blk, _mm(q_att[blk, cols], k_diag), 0.0).astype(BF16)
        last = (g + 1) * GLA_BLOCK - 1
        total = l1[last:last + 1, cols] + l2[last:last + 1, cols]
        decay_row = jnp.broadcast_to(jnp.exp(total), (LANES, LANES))
        decay_col[p, g] = jnp.sum(jnp.where(eye, decay_row, 0.0), axis=1, keepdims=True)

    gate_a = jax.nn.silu(jnp.concatenate(gate_parts, axis=1))

    states = {}
    for p in range(n_pair):
        state = s_ref[p * LANES:(p + 1) * LANES, :]
        for g in range(n_gblk):
            states[p, g] = state.astype(BF16)
            state = state * decay_col[p, g] + upd[p, g]
        s_ref[p * LANES:(p + 1) * LANES, :] = state

    o_swa = {}
    gate_b_parts.append(proj(GB + 2 * LANES, 2 * LANES))
    gate_b = jax.nn.silu(jnp.concatenate(gate_b_parts, axis=1))
    for h, i in swa_tiles:
        kv_rows = slice(i * BLOCK, (i + 2) * BLOCK)
        o_swa[h, i] = _mm(v_t[h * HALF:(h + 1) * HALF, kv_rows], probs[h, i])
    zero_vb = jnp.zeros((GLA_BLOCK, GLA_DV), BF16)
    zero_vs = jnp.zeros((GLA_DK, GLA_DV), BF16)
    o_gla = {}
    for p, g in gla_tiles:
        blk = slice(g * GLA_BLOCK, (g + 1) * GLA_BLOCK)
        v2 = v_b[blk, p * pair_w:(p + 1) * pair_w]
        st = states[p, g]
        rhs = jnp.concatenate([
            jnp.concatenate([v2[:, :GLA_DV], zero_vb], axis=1),
            jnp.concatenate([zero_vb, v2[:, GLA_DV:]], axis=1),
            jnp.concatenate([st[:GLA_DK], zero_vs], axis=1),
            jnp.concatenate([zero_vs, st[GLA_DK:]], axis=1)], axis=0)
        lhs = jnp.concatenate([attn[p, g], q_dec[blk, p * LANES:(p + 1) * LANES]], axis=1)
        o_gla[p, g] = _mm(lhs, rhs)

    for h, i in swa_tiles:
        rows = slice(i * BLOCK, (i + 1) * BLOCK)
        o_t, inv = o_swa[h, i], inv_den[h, i]
        for c in range(2):
            cg = 2 * h + c
            o2_t = jnp.concatenate(
                [o_t[:, (2 * c + par) * BLOCK:(2 * c + par + 1) * BLOCK] * inv[2 * c + par]
                 for par in range(2)], axis=0)
            cg_cols = slice(cg * LANES, (cg + 1) * LANES)
            out = o2_t.T * gate_a[rows, cg_cols]
            mix_ref[rows, cg_cols] = out.astype(BF16)
    norm_w = nw_ref[...]
    for p, g in gla_tiles:
        rows = slice(g * GLA_BLOCK, (g + 1) * GLA_BLOCK)
        for par in range(2):
            o = o_gla[p, g][:, par * GLA_DV:(par + 1) * GLA_DV]
            o = o * lax.rsqrt(jnp.mean(jnp.square(o), axis=-1, keepdims=True) + EPS) * norm_w
            c0 = (2 * p + par) * GLA_DV
            out = o * gate_b[rows, c0:c0 + GLA_DV]
            mix_ref[rows, SWA_WIDTH + c0:SWA_WIDTH + c0 + GLA_DV] = out.astype(BF16)

    for r in range(0, TILE, OUT_ROWS):
        rows = slice(r, r + OUT_ROWS)
        y = _mm(mix_ref[rows, :], w_out_ref[...]) + ALPHA * x_ref[rows, :]
        mu = jnp.mean(y, axis=-1, keepdims=True)
        yc = y - mu
        var = jnp.mean(jnp.square(yc), axis=-1, keepdims=True)
        o_ref[rows, :] = yc * lax.rsqrt(var + EPS) * lng_ref[...] + lnb_ref[...]


def _rope_freq_rows():
    half = ROT_DIM // 2
    inv_freq = ROPE_THETA ** (-jnp.arange(half, dtype=F32) / half)
    return jnp.broadcast_to(inv_freq[:, None], (half, BLOCK))


@jax.jit
def kernel(x, positions, w_in, gla_w_gate_up, gla_b_gate, attn_sinks, gla_norm_w, w_out, ln_g, ln_b):
    B, S, D = x.shape
    assert D == D_MODEL and S % TILE == 0 and w_in.shape == (DEPTH, D_MODEL, D_IN_PROJ)
    n_tiles = S // TILE
    pos_rows = positions.reshape(B * n_tiles, 1, TILE)

    def const(shape):
        return pl.BlockSpec((None,) + shape, lambda b, t: (0,) * (len(shape) + 1),
                            pipeline_mode=pl.Buffered(1))

    grid_spec = pl.GridSpec(
        grid=(B, n_tiles),
        in_specs=[
            pl.BlockSpec((None, TILE, D_MODEL), lambda b, t: (b, t, 0)),
            pl.BlockSpec((None, 1, TILE), lambda b, t: (b * n_tiles + t, 0, 0)),
            const((D_IN_PROJ, D_MODEL)),
            const((GLA_RANK, GLA_HEADS * GLA_DK)),
            const((1, GLA_HEADS * GLA_DK)),
            const((1, GLA_DV)),
            const((D_MODEL, D_MODEL)),
            const((1, D_MODEL)),
            const((1, D_MODEL)),
            pl.BlockSpec((ROT_DIM // 2, BLOCK), lambda b, t: (0, 0)),
            pl.BlockSpec(memory_space=pltpu.SMEM),
        ],
        out_specs=pl.BlockSpec((None, TILE, D_MODEL), lambda b, t: (b, t, 0)),
        scratch_shapes=[
            pltpu.VMEM((BLOCK, LANES), F32),
            pltpu.VMEM((BLOCK, LANES), F32),
            pltpu.VMEM((GLA_HEADS * GLA_DK, GLA_DV), F32),
            pltpu.VMEM((TILE, D_MODEL), BF16),
            pltpu.VMEM((RB // LANES, D_MODEL, LANES), BF16),
            pltpu.VMEM((D_MODEL, LANES), BF16),
            pltpu.VMEM((LANES, GLA_HEADS * GLA_DK), BF16),
            pltpu.VMEM((D_MODEL, D_MODEL), BF16),
            pltpu.VMEM((CUM_ROWS, CUM_ROWS), BF16),
        ],
    )
    row = lambda a: a.reshape(DEPTH, 1, -1)
    return pl.pallas_call(
        _layer_kernel,
        grid_spec=grid_spec,
        out_shape=jax.ShapeDtypeStruct((B, S, D_MODEL), x.dtype),
        compiler_params=pltpu.CompilerParams(
            dimension_semantics=("arbitrary", "arbitrary"),
            vmem_limit_bytes=VMEM_LIMIT_BYTES),
        name="hymba_layer",
    )(x, pos_rows, jnp.swapaxes(w_in, 1, 2), gla_w_gate_up, row(gla_b_gate), row(gla_norm_w),
      w_out, row(ln_g), row(ln_b), _rope_freq_rows(), attn_sinks[0])
```

```python
import jax
import jax.numpy as jnp
from jax import lax
from jax.experimental import pallas as pl
from jax.experimental.pallas import tpu as pltpu

D_MODEL = 1024
SWA_WIDTH = 512
SWA_HEAD_DIM = 64
SWA_Q_HEADS = 8
SWA_KV_HEADS = 2
WINDOW = 128
BLOCK = 128
ROPE_THETA = 500000.0
ROT_DIM = 16
GLA_WIDTH = 512
GLA_HEADS = 4
GLA_DK = 64
GLA_DV = 128
GLA_RANK = 16
GLA_TAU = 16.0
GLA_CHUNK = 64
EPS = 1e-5
DEPTH = 1
ALPHA = (2 * DEPTH) ** 0.25

LANES = 128
HALF = LANES // 2

QA, KA, VA, GA, QB, KB, VB, GB, RB = 0, 512, 640, 768, 1280, 1536, 1792, 2304, 2816
D_IN_PROJ = RB + GLA_RANK

TILE = 512
CUM_ROWS = 256
OUT_ROWS = 256
GLA_BLOCK = 2 * GLA_CHUNK
FILL_ROWS = 256
FILL_PER_SCORE_TILE = (1, 0)
FILL_PER_GLA_DOT = (1, 0, 0, 0)
W_CAST_ROWS = 128
VMEM_LIMIT_BYTES = 60 * 1024 * 1024

F32 = jnp.float32
BF16 = jnp.bfloat16


def _nt(a, b):
    return lax.dot_general(a, b, (((1,), (1,)), ((), ())), preferred_element_type=F32)


def _mm(a, b):
    return jnp.dot(a, b, preferred_element_type=F32)


def _layer_kernel(x_ref, pos_ref, w_in_t_f32, w_up_f32, bg_ref, nw_ref, w_out_f32,
                  lng_ref, lnb_ref, invf_ref, sinks_ref,
                  o_ref, kprev_ref, vprev_ref, s_ref, mix_ref,
                  w_in_ref, w_r_ref, w_up_ref, w_out_ref, tri_ref):
    t_idx = pl.program_id(1)

    @pl.when((pl.program_id(0) == 0) & (t_idx == 0))
    def _cast_weights():
        def cast_group(g, carry):
            rows = pl.ds(pl.multiple_of(g * LANES, LANES), LANES)
            w_in_ref[g] = w_in_t_f32[rows, :].T.astype(BF16)
            return carry
        lax.fori_loop(0, RB // LANES, cast_group, 0)
        rank_rows = jnp.concatenate(
            [w_in_t_f32[RB:, :], jnp.zeros((LANES - GLA_RANK, D_MODEL), F32)], axis=0)
        w_r_ref[...] = rank_rows.T.astype(BF16)
        ri = lax.broadcasted_iota(jnp.int32, (CUM_ROWS, CUM_ROWS), 0)
        ci = lax.broadcasted_iota(jnp.int32, (CUM_ROWS, CUM_ROWS), 1)
        tri_ref[...] = ((ri // GLA_CHUNK == ci // GLA_CHUNK) & (ci <= ri)).astype(BF16)

        def cast_rows(r, carry):
            rows = pl.ds(pl.multiple_of(r * W_CAST_ROWS, W_CAST_ROWS), W_CAST_ROWS)
            w_out_ref[rows, :] = w_out_f32[rows, :].astype(BF16)
            return carry
        lax.fori_loop(0, D_MODEL // W_CAST_ROWS, cast_rows, 0)
        w_up_ref[...] = jnp.concatenate(
            [w_up_f32[...], jnp.zeros((LANES - GLA_RANK, GLA_HEADS * GLA_DK), F32)],
            axis=0).astype(BF16)

    @pl.when(t_idx == 0)
    def _reset_carries():
        kprev_ref[...] = jnp.zeros_like(kprev_ref)
        vprev_ref[...] = jnp.zeros_like(vprev_ref)
        s_ref[...] = jnp.zeros_like(s_ref)

    xb = x_ref[...].astype(BF16)

    def proj(lo, width, rows=slice(None)):
        groups = [w_in_ref[g] for g in range(lo // LANES, (lo + width) // LANES)]
        return _mm(xb[rows], jnp.concatenate(groups, axis=1))

    fill_piece = 2 * LANES
    pending = [(lo, c, r) for lo in (VB, GA, GB)
               for r in range(0, TILE, FILL_ROWS) for c in range(0, GLA_WIDTH, fill_piece)]
    pieces = {}

    def fill(n=1):
        for _ in range(n):
            if pending:
                lo, c, r = pending.pop(0)
                pieces[lo, c, r] = proj(lo + c, fill_piece, slice(r, r + FILL_ROWS))

    def filled(lo):
        return jnp.concatenate(
            [jnp.concatenate([pieces[lo, c, r] for c in range(0, GLA_WIDTH, fill_piece)], axis=1)
             for r in range(0, TILE, FILL_ROWS)], axis=0)

    lane = lax.broadcasted_iota(jnp.int32, (1, LANES), 1)
    lo_half = lane < HALF
    n_blk = TILE // BLOCK
    n_gblk = TILE // GLA_BLOCK
    n_pair = GLA_HEADS // 2
    pair_w = 2 * GLA_DV
    gla_lanes = GLA_HEADS * GLA_DK

    r_b = _mm(xb, w_r_ref[...]).astype(BF16)
    kv_new = proj(KA, 2 * LANES)
    q_a = proj(QA, SWA_WIDTH)
    logit = _mm(r_b, w_up_ref[...]) + bg_ref[...]
    log_a = jax.nn.log_sigmoid(logit) / GLA_TAU
    qk_b = proj(QB, 2 * gla_lanes)
    q_b = qk_b[:, :gla_lanes]
    k_b = qk_b[:, gla_lanes:]

    tri = tri_ref[...]
    la_hi = log_a.astype(BF16)
    la_lo = (log_a - la_hi.astype(F32)).astype(BF16)
    b = jnp.concatenate(
        [_mm(tri, la_hi[r:r + CUM_ROWS]) + _mm(tri, la_lo[r:r + CUM_ROWS])
         for r in range(0, TILE, CUM_ROWS)], axis=0)

    half = ROT_DIM // 2
    pos = pos_ref[...].astype(F32)
    ones_r = jnp.ones((HALF - ROT_DIM, BLOCK), F32)
    zeros_r = jnp.zeros((HALF - ROT_DIM, BLOCK), F32)
    zeros_h = jnp.zeros((half, BLOCK), F32)
    cos_blk, up_blk, dn_blk = [], [], []
    for i in range(TILE // BLOCK):
        ang = invf_ref[...] * pos[:, i * BLOCK:(i + 1) * BLOCK]
        c8, s8 = jnp.cos(ang), jnp.sin(ang)
        cos_blk.append(jnp.concatenate([c8, c8, ones_r] * 2, axis=0).T)
        up_blk.append(jnp.concatenate([zeros_h, s8, zeros_r] * 2, axis=0).T)
        dn_blk.append(jnp.concatenate([-s8, zeros_h, zeros_r] * 2, axis=0).T)
    cos = jnp.concatenate(cos_blk, axis=0)
    sin_up = jnp.concatenate(up_blk, axis=0)
    sin_dn = jnp.concatenate(dn_blk, axis=0)

    def rope(t):
        return (t * cos + pltpu.roll(t, ROT_DIM // 2, 1) * sin_up
                + pltpu.roll(t, LANES - ROT_DIM // 2, 1) * sin_dn)

    k_new = rope(kv_new[:, :LANES])
    v_new = kv_new[:, LANES:]
    k_cat = jnp.concatenate([kprev_ref[...], k_new], axis=0)
    v_cat = jnp.concatenate([vprev_ref[...], v_new], axis=0)
    kprev_ref[...] = k_new[TILE - BLOCK:]
    vprev_ref[...] = v_new[TILE - BLOCK:]

    k_sw = pltpu.roll(k_cat, HALF, 1)
    k_cb, k_sb = k_cat.astype(BF16), k_sw.astype(BF16)
    k_zero = jnp.zeros_like(k_cb)
    k_even = [jnp.where(lo_half, k_cb, k_zero), jnp.where(lo_half, k_sb, k_zero)]
    k_odd = [jnp.where(lo_half, k_zero, k_sb), jnp.where(lo_half, k_zero, k_cb)]
    v_t = v_cat.T.astype(BF16)

    scale = SWA_HEAD_DIM ** -0.5
    q_t = [(rope(q_a[:, cg * LANES:(cg + 1) * LANES]) * scale).T.astype(BF16)
           for cg in range(SWA_WIDTH // LANES)]

    swa_tiles = [(h, i) for i in range(n_blk) for h in range(SWA_KV_HEADS)]
    scores = {}
    for n, (h, i) in enumerate(swa_tiles):
        fill(FILL_PER_SCORE_TILE[n % len(FILL_PER_SCORE_TILE)])
        rows = slice(i * BLOCK, (i + 1) * BLOCK)
        kv_rows = slice(i * BLOCK, (i + 2) * BLOCK)
        q_pair = jnp.concatenate([q_t[2 * h][:, rows], q_t[2 * h + 1][:, rows]], axis=1)
        scores[h, i] = [_mm(k_even[h][kv_rows], q_pair),
                        _mm(k_odd[h][kv_rows], q_pair)]
    v_b = filled(VB).astype(BF16)

    kj = lax.broadcasted_iota(jnp.int32, (2 * BLOCK, BLOCK), 0)
    qj = lax.broadcasted_iota(jnp.int32, (2 * BLOCK, BLOCK), 1)
    in_window = (kj > qj) & (kj <= qj + WINDOW)
    first_blk_mask = in_window & ((kj >= BLOCK) | (t_idx > 0))
    probs, inv_den = {}, {}
    for h, i in swa_tiles:
        mask = first_blk_mask if i == 0 else in_window
        p_t, inv = [], []
        for g in range(4):
            sink = sinks_ref[4 * h + g]
            s = scores[h, i][g % 2][:, (g // 2) * BLOCK:(g // 2 + 1) * BLOCK]
            s = jnp.where(mask, s, -jnp.inf)
            m = jnp.maximum(jnp.max(s, axis=0, keepdims=True), sink)
            e = jnp.exp(s - m)
            denom = jnp.sum(e, axis=0, keepdims=True) + jnp.exp(sink - m)
            inv.append(1.0 / denom)
            p_t.append(e.astype(BF16))
        probs[h, i] = jnp.concatenate(p_t, axis=1)
        inv_den[h, i] = inv

    def chunk_end_rows(c):
        return jnp.broadcast_to(b[(c + 1) * GLA_CHUNK - 1:(c + 1) * GLA_CHUNK, :], (GLA_CHUNK, gla_lanes))

    zeros_c = jnp.zeros((GLA_CHUNK, gla_lanes), F32)
    l1 = jnp.concatenate([chunk_end_rows(2 * g) for g in range(n_gblk) for _ in range(2)], axis=0)
    l1_c1 = jnp.concatenate([blk for g in range(n_gblk) for blk in (chunk_end_rows(2 * g), zeros_c)], axis=0)
    l1_c2 = l1 - l1_c1
    l2 = jnp.concatenate([chunk_end_rows(2 * g + 1) for g in range(n_gblk) for _ in range(2)], axis=0)
    pivot = b - l1_c1
    q_s = q_b * (GLA_DK ** -0.5)
    q_att = (q_s * jnp.exp(pivot)).astype(BF16)
    k_att_t = (k_b * jnp.exp(-pivot)).T.astype(BF16)
    q_dec = (q_s * jnp.exp(b + l1_c2)).astype(BF16)
    k_end_t = (k_b * jnp.exp(l1_c1 - b + l2)).T.astype(BF16)

    causal_blk = (lax.broadcasted_iota(jnp.int32, (GLA_BLOCK, 2 * GLA_BLOCK), 1) & (GLA_BLOCK - 1)
                  ) <= lax.broadcasted_iota(jnp.int32, (GLA_BLOCK, 2 * GLA_BLOCK), 0)
    eye = (lax.broadcasted_iota(jnp.int32, (LANES, LANES), 0)
           == lax.broadcasted_iota(jnp.int32, (LANES, LANES), 1))
    gla_tiles = [(p, g) for g in range(n_gblk) for p in range(n_pair)]
    zero_kt = jnp.zeros((GLA_DK, GLA_BLOCK), BF16)
    attn, upd, decay_col = {}, {}, {}
    for n, (p, g) in enumerate(gla_tiles):
        fill(FILL_PER_GLA_DOT[n % len(FILL_PER_GLA_DOT)])
        cols = slice(p * LANES, (p + 1) * LANES)
        blk = slice(g * GLA_BLOCK, (g + 1) * GLA_BLOCK)
        u = _mm(k_end_t[cols, blk], v_b[blk, p * pair_w:(p + 1) * pair_w])
        upd[p, g] = jnp.concatenate([u[:GLA_DK, :GLA_DV], u[GLA_DK:, GLA_DV:]], axis=0)
    for n, (p, g) in enumerate(gla_tiles):
        fill(FILL_PER_GLA_DOT[n % len(FILL_PER_GLA_DOT)])
        cols = slice(p * LANES, (p + 1) * LANES)
        blk = slice(g * GLA_BLOCK, (g + 1) * GLA_BLOCK)
        ka2 = k_att_t[cols, blk]
        k_diag = jnp.concatenate([jnp.concatenate([ka2[:GLA_DK], zero_kt], axis=1),
                                  jnp.concatenate([zero_kt, ka2[GLA_DK:]], axis=1)], axis=0)
        attn[p, g] = jnp.where(causal_blk, _mm(q_att[blk, cols], k_diag), 0.0).astype(BF16)
        last = (g + 1) * GLA_BLOCK - 1
        total = l1[last:last + 1, cols] + l2[last:last + 1, cols]
        decay_row = jnp.broadcast_to(jnp.exp(total), (LANES, LANES))
        decay_col[p, g] = jnp.sum(jnp.where(eye, decay_row, 0.0), axis=1, keepdims=True)

    states = {}
    for p in range(n_pair):
        state = s_ref[p * LANES:(p + 1) * LANES, :]
        for g in range(n_gblk):
            states[p, g] = state.astype(BF16)
            state = state * decay_col[p, g] + upd[p, g]
        s_ref[p * LANES:(p + 1) * LANES, :] = state

    o_swa = {}
    for n, (h, i) in enumerate(swa_tiles):
        fill(FILL_PER_GLA_DOT[n % len(FILL_PER_GLA_DOT)])
        kv_rows = slice(i * BLOCK, (i + 2) * BLOCK)
        o_swa[h, i] = _mm(v_t[h * HALF:(h + 1) * HALF, kv_rows], probs[h, i])
    zero_vb = jnp.zeros((GLA_BLOCK, GLA_DV), BF16)
    zero_vs = jnp.zeros((GLA_DK, GLA_DV), BF16)
    o_gla = {}
    for n, (p, g) in enumerate(gla_tiles):
        fill(FILL_PER_GLA_DOT[n % len(FILL_PER_GLA_DOT)])
        blk = slice(g * GLA_BLOCK, (g + 1) * GLA_BLOCK)
        v2 = v_b[blk, p * pair_w:(p + 1) * pair_w]
        st = states[p, g]
        rhs = jnp.concatenate([
            jnp.concatenate([v2[:, :GLA_DV], zero_vb], axis=1),
            jnp.concatenate([zero_vb, v2[:, GLA_DV:]], axis=1),
            jnp.concatenate([st[:GLA_DK], zero_vs], axis=1),
            jnp.concatenate([zero_vs, st[GLA_DK:]], axis=1)], axis=0)
        lhs = jnp.concatenate([attn[p, g], q_dec[blk, p * LANES:(p + 1) * LANES]], axis=1)
        o_gla[p, g] = _mm(lhs, rhs)

    fill(len(pending))
    gate_a = jax.nn.silu(filled(GA))
    gate_b = jax.nn.silu(filled(GB))
    for h, i in swa_tiles:
        rows = slice(i * BLOCK, (i + 1) * BLOCK)
        o_t, inv = o_swa[h, i], inv_den[h, i]
        for c in range(2):
            cg = 2 * h + c
            o2_t = jnp.concatenate(
                [o_t[:, (2 * c + par) * BLOCK:(2 * c + par + 1) * BLOCK] * inv[2 * c + par]
                 for par in range(2)], axis=0)
            cg_cols = slice(cg * LANES, (cg + 1) * LANES)
            out = o2_t.T * gate_a[rows, cg_cols]
            mix_ref[rows, cg_cols] = out.astype(BF16)
    norm_w = nw_ref[...]
    for p, g in gla_tiles:
        rows = slice(g * GLA_BLOCK, (g + 1) * GLA_BLOCK)
        for par in range(2):
            o = o_gla[p, g][:, par * GLA_DV:(par + 1) * GLA_DV]
            o = o * lax.rsqrt(jnp.mean(jnp.square(o), axis=-1, keepdims=True) + EPS) * norm_w
            c0 = (2 * p + par) * GLA_DV
            out = o * gate_b[rows, c0:c0 + GLA_DV]
            mix_ref[rows, SWA_WIDTH + c0:SWA_WIDTH + c0 + GLA_DV] = out.astype(BF16)

    for r in range(0, TILE, OUT_ROWS):
        rows = slice(r, r + OUT_ROWS)
        y = _mm(mix_ref[rows, :], w_out_ref[...]) + ALPHA * x_ref[rows, :]
        mu = jnp.mean(y, axis=-1, keepdims=True)
        yc = y - mu
        var = jnp.mean(jnp.square(yc), axis=-1, keepdims=True)
        o_ref[rows, :] = yc * lax.rsqrt(var + EPS) * lng_ref[...] + lnb_ref[...]


def _rope_freq_rows():
    half = ROT_DIM // 2
    inv_freq = ROPE_THETA ** (-jnp.arange(half, dtype=F32) / half)
    return jnp.broadcast_to(inv_freq[:, None], (half, BLOCK))


@jax.jit
def kernel(x, positions, w_in, gla_w_gate_up, gla_b_gate, attn_sinks, gla_norm_w, w_out, ln_g, ln_b):
    B, S, D = x.shape
    assert D == D_MODEL and S % TILE == 0 and w_in.shape == (DEPTH, D_MODEL, D_IN_PROJ)
    n_tiles = S // TILE
    pos_rows = positions.reshape(B * n_tiles, 1, TILE)

    def const(shape):
        return pl.BlockSpec((None,) + shape, lambda b, t: (0,) * (len(shape) + 1),
                            pipeline_mode=pl.Buffered(1))

    grid_spec = pl.GridSpec(
        grid=(B, n_tiles),
        in_specs=[
            pl.BlockSpec((None, TILE, D_MODEL), lambda b, t: (b, t, 0)),
            pl.BlockSpec((None, 1, TILE), lambda b, t: (b * n_tiles + t, 0, 0)),
            const((D_IN_PROJ, D_MODEL)),
            const((GLA_RANK, GLA_HEADS * GLA_DK)),
            const((1, GLA_HEADS * GLA_DK)),
            const((1, GLA_DV)),
            const((D_MODEL, D_MODEL)),
            const((1, D_MODEL)),
            const((1, D_MODEL)),
            pl.BlockSpec((ROT_DIM // 2, BLOCK), lambda b, t: (0, 0)),
            pl.BlockSpec(memory_space=pltpu.SMEM),
        ],
        out_specs=pl.BlockSpec((None, TILE, D_MODEL), lambda b, t: (b, t, 0)),
        scratch_shapes=[
            pltpu.VMEM((BLOCK, LANES), F32),
            pltpu.VMEM((BLOCK, LANES), F32),
            pltpu.VMEM((GLA_HEADS * GLA_DK, GLA_DV), F32),
            pltpu.VMEM((TILE, D_MODEL), BF16),
            pltpu.VMEM((RB // LANES, D_MODEL, LANES), BF16),
            pltpu.VMEM((D_MODEL, LANES), BF16),
            pltpu.VMEM((LANES, GLA_HEADS * GLA_DK), BF16),
            pltpu.VMEM((D_MODEL, D_MODEL), BF16),
            pltpu.VMEM((CUM_ROWS, CUM_ROWS), BF16),
        ],
    )
    row = lambda a: a.reshape(DEPTH, 1, -1)
    return pl.pallas_call(
        _layer_kernel,
        grid_spec=grid_spec,
        out_shape=jax.ShapeDtypeStruct((B, S, D_MODEL), x.dtype),
        compiler_params=pltpu.CompilerParams(
            dimension_semantics=("arbitrary", "arbitrary"),
            vmem_limit_bytes=VMEM_LIMIT_BYTES),
        name="hymba_layer",
    )(x, pos_rows, jnp.swapaxes(w_in, 1, 2), gla_w_gate_up, row(gla_b_gate), row(gla_norm_w),
      w_out, row(ln_g), row(ln_b), _rope_freq_rows(), attn_sinks[0])
```

```python
import jax
import jax.numpy as jnp
from jax import lax
from jax.experimental import pallas as pl
from jax.experimental.pallas import tpu as pltpu

D_MODEL = 1024
SWA_WIDTH = 512
SWA_HEAD_DIM = 64
SWA_Q_HEADS = 8
SWA_KV_HEADS = 2
WINDOW = 128
BLOCK = 128
ROPE_THETA = 500000.0
ROT_DIM = 16
GLA_WIDTH = 512
GLA_HEADS = 4
GLA_DK = 64
GLA_DV = 128
GLA_RANK = 16
GLA_TAU = 16.0
GLA_CHUNK = 64
EPS = 1e-5
DEPTH = 1
ALPHA = (2 * DEPTH) ** 0.25

LANES = 128
HALF = LANES // 2

QA, KA, VA, GA, QB, KB, VB, GB, RB = 0, 512, 640, 768, 1280, 1536, 1792, 2304, 2816
D_IN_PROJ = RB + GLA_RANK

TILE = 512
CUM_ROWS = 256
OUT_ROWS = 256
GLA_BLOCK = 2 * GLA_CHUNK
FILL_ROWS = TILE
FILL_ORDER = ((GB, 0), (VB, 0), (VB, 256), (GA, 0), (GA, 256), (GB, 256))
FILL_BEFORE_SCORES = 1
FILL_PER_SCORE_TILE = (1, 0, 0, 0)
FILL_PER_GLA_DOT = (1, 0, 0, 0, 0, 0, 0, 0)
W_CAST_ROWS = 128
VMEM_LIMIT_BYTES = 60 * 1024 * 1024

F32 = jnp.float32
BF16 = jnp.bfloat16


def _nt(a, b):
    return lax.dot_general(a, b, (((1,), (1,)), ((), ())), preferred_element_type=F32)


def _mm(a, b):
    return jnp.dot(a, b, preferred_element_type=F32)


def _layer_kernel(x_ref, pos_ref, w_in_t_f32, w_up_f32, bg_ref, nw_ref, w_out_f32,
                  lng_ref, lnb_ref, invf_ref, sinks_ref,
                  o_ref, kprev_ref, vprev_ref, s_ref, mix_ref,
                  w_in_ref, w_r_ref, w_up_ref, w_out_ref, tri_ref):
    t_idx = pl.program_id(1)

    @pl.when((pl.program_id(0) == 0) & (t_idx == 0))
    def _cast_weights():
        def cast_group(g, carry):
            rows = pl.ds(pl.multiple_of(g * LANES, LANES), LANES)
            w_in_ref[g] = w_in_t_f32[rows, :].T.astype(BF16)
            return carry
        lax.fori_loop(0, RB // LANES, cast_group, 0)
        rank_rows = jnp.concatenate(
            [w_in_t_f32[RB:, :], jnp.zeros((LANES - GLA_RANK, D_MODEL), F32)], axis=0)
        w_r_ref[...] = rank_rows.T.astype(BF16)
        ri = lax.broadcasted_iota(jnp.int32, (CUM_ROWS, CUM_ROWS), 0)
        ci = lax.broadcasted_iota(jnp.int32, (CUM_ROWS, CUM_ROWS), 1)
        tri_ref[...] = ((ri // GLA_CHUNK == ci // GLA_CHUNK) & (ci <= ri)).astype(BF16)

        def cast_rows(r, carry):
            rows = pl.ds(pl.multiple_of(r * W_CAST_ROWS, W_CAST_ROWS), W_CAST_ROWS)
            w_out_ref[rows, :] = w_out_f32[rows, :].astype(BF16)
            return carry
        lax.fori_loop(0, D_MODEL // W_CAST_ROWS, cast_rows, 0)
        w_up_ref[...] = jnp.concatenate(
            [w_up_f32[...], jnp.zeros((LANES - GLA_RANK, GLA_HEADS * GLA_DK), F32)],
            axis=0).astype(BF16)

    @pl.when(t_idx == 0)
    def _reset_carries():
        kprev_ref[...] = jnp.zeros_like(kprev_ref)
        vprev_ref[...] = jnp.zeros_like(vprev_ref)
        s_ref[...] = jnp.zeros_like(s_ref)

    xb = x_ref[...].astype(BF16)

    def proj(lo, width, rows=slice(None)):
        groups = [w_in_ref[g] for g in range(lo // LANES, (lo + width) // LANES)]
        return _mm(xb[rows], jnp.concatenate(groups, axis=1))

    fill_piece = 2 * LANES
    pending = [(lo, c, r) for lo, c in FILL_ORDER for r in range(0, TILE, FILL_ROWS)]
    pieces = {}

    def fill(n=1):
        for _ in range(n):
            if pending:
                lo, c, r = pending.pop(0)
                pieces[lo, c, r] = proj(lo + c, fill_piece, slice(r, r + FILL_ROWS))

    def filled(lo):
        return jnp.concatenate(
            [jnp.concatenate([pieces[lo, c, r] for c in range(0, GLA_WIDTH, fill_piece)], axis=1)
             for r in range(0, TILE, FILL_ROWS)], axis=0)

    lane = lax.broadcasted_iota(jnp.int32, (1, LANES), 1)
    lo_half = lane < HALF
    n_blk = TILE // BLOCK
    n_gblk = TILE // GLA_BLOCK
    n_pair = GLA_HEADS // 2
    pair_w = 2 * GLA_DV
    gla_lanes = GLA_HEADS * GLA_DK

    r_b = _mm(xb, w_r_ref[...]).astype(BF16)
    kv_new = proj(KA, 2 * LANES)
    q_a = proj(QA, SWA_WIDTH)
    logit = _mm(r_b, w_up_ref[...]) + bg_ref[...]
    log_a = jax.nn.log_sigmoid(logit) / GLA_TAU
    qk_b = proj(QB, 2 * gla_lanes)
    q_b = qk_b[:, :gla_lanes]
    k_b = qk_b[:, gla_lanes:]

    tri = tri_ref[...]
    la_hi = log_a.astype(BF16)
    la_lo = (log_a - la_hi.astype(F32)).astype(BF16)
    b = jnp.concatenate(
        [_mm(tri, la_hi[r:r + CUM_ROWS]) + _mm(tri, la_lo[r:r + CUM_ROWS])
         for r in range(0, TILE, CUM_ROWS)], axis=0)

    half = ROT_DIM // 2
    pos = pos_ref[...].astype(F32)
    ones_r = jnp.ones((HALF - ROT_DIM, BLOCK), F32)
    zeros_r = jnp.zeros((HALF - ROT_DIM, BLOCK), F32)
    zeros_h = jnp.zeros((half, BLOCK), F32)
    cos_blk, up_blk, dn_blk = [], [], []
    for i in range(TILE // BLOCK):
        ang = invf_ref[...] * pos[:, i * BLOCK:(i + 1) * BLOCK]
        c8, s8 = jnp.cos(ang), jnp.sin(ang)
        cos_blk.append(jnp.concatenate([c8, c8, ones_r] * 2, axis=0).T)
        up_blk.append(jnp.concatenate([zeros_h, s8, zeros_r] * 2, axis=0).T)
        dn_blk.append(jnp.concatenate([-s8, zeros_h, zeros_r] * 2, axis=0).T)
    cos = jnp.concatenate(cos_blk, axis=0)
    sin_up = jnp.concatenate(up_blk, axis=0)
    sin_dn = jnp.concatenate(dn_blk, axis=0)

    def rope(t):
        return (t * cos + pltpu.roll(t, ROT_DIM // 2, 1) * sin_up
                + pltpu.roll(t, LANES - ROT_DIM // 2, 1) * sin_dn)

    k_new = rope(kv_new[:, :LANES])
    v_new = kv_new[:, LANES:]
    k_cat = jnp.concatenate([kprev_ref[...], k_new], axis=0)
    v_cat = jnp.concatenate([vprev_ref[...], v_new], axis=0)
    kprev_ref[...] = k_new[TILE - BLOCK:]
    vprev_ref[...] = v_new[TILE - BLOCK:]

    k_sw = pltpu.roll(k_cat, HALF, 1)
    k_cb, k_sb = k_cat.astype(BF16), k_sw.astype(BF16)
    k_zero = jnp.zeros_like(k_cb)
    k_even = [jnp.where(lo_half, k_cb, k_zero), jnp.where(lo_half, k_sb, k_zero)]
    k_odd = [jnp.where(lo_half, k_zero, k_sb), jnp.where(lo_half, k_zero, k_cb)]
    v_t = v_cat.T.astype(BF16)

    scale = SWA_HEAD_DIM ** -0.5
    q_t = [(rope(q_a[:, cg * LANES:(cg + 1) * LANES]) * scale).T.astype(BF16)
           for cg in range(SWA_WIDTH // LANES)]

    swa_tiles = [(h, i) for i in range(n_blk) for h in range(SWA_KV_HEADS)]
    scores = {}
    fill(FILL_BEFORE_SCORES)
    for n, (h, i) in enumerate(swa_tiles):
        fill(FILL_PER_SCORE_TILE[n % len(FILL_PER_SCORE_TILE)])
        rows = slice(i * BLOCK, (i + 1) * BLOCK)
        kv_rows = slice(i * BLOCK, (i + 2) * BLOCK)
        q_pair = jnp.concatenate([q_t[2 * h][:, rows], q_t[2 * h + 1][:, rows]], axis=1)
        scores[h, i] = [_mm(k_even[h][kv_rows], q_pair),
                        _mm(k_odd[h][kv_rows], q_pair)]
    v_b = filled(VB).astype(BF16)

    kj = lax.broadcasted_iota(jnp.int32, (2 * BLOCK, BLOCK), 0)
    qj = lax.broadcasted_iota(jnp.int32, (2 * BLOCK, BLOCK), 1)
    in_window = (kj > qj) & (kj <= qj + WINDOW)
    first_blk_mask = in_window & ((kj >= BLOCK) | (t_idx > 0))
    probs, inv_den = {}, {}
    for h, i in swa_tiles:
        mask = first_blk_mask if i == 0 else in_window
        p_t, inv = [], []
        for g in range(4):
            sink = sinks_ref[4 * h + g]
            s = scores[h, i][g % 2][:, (g // 2) * BLOCK:(g // 2 + 1) * BLOCK]
            s = jnp.where(mask, s, -jnp.inf)
            m = jnp.maximum(jnp.max(s, axis=0, keepdims=True), sink)
            e = jnp.exp(s - m)
            denom = jnp.sum(e, axis=0, keepdims=True) + jnp.exp(sink - m)
            inv.append(1.0 / denom)
            p_t.append(e.astype(BF16))
        probs[h, i] = jnp.concatenate(p_t, axis=1)
        inv_den[h, i] = inv

    def chunk_end_rows(c):
        return jnp.broadcast_to(b[(c + 1) * GLA_CHUNK - 1:(c + 1) * GLA_CHUNK, :], (GLA_CHUNK, gla_lanes))

    zeros_c = jnp.zeros((GLA_CHUNK, gla_lanes), F32)
    l1 = jnp.concatenate([chunk_end_rows(2 * g) for g in range(n_gblk) for _ in range(2)], axis=0)
    l1_c1 = jnp.concatenate([blk for g in range(n_gblk) for blk in (chunk_end_rows(2 * g), zeros_c)], axis=0)
    l1_c2 = l1 - l1_c1
    l2 = jnp.concatenate([chunk_end_rows(2 * g + 1) for g in range(n_gblk) for _ in range(2)], axis=0)
    pivot = b - l1_c1
    q_s = q_b * (GLA_DK ** -0.5)
    q_att = (q_s * jnp.exp(pivot)).astype(BF16)
    k_att_t = (k_b * jnp.exp(-pivot)).T.astype(BF16)
    q_dec = (q_s * jnp.exp(b + l1_c2)).astype(BF16)
    k_end_t = (k_b * jnp.exp(l1_c1 - b + l2)).T.astype(BF16)

    causal_blk = (lax.broadcasted_iota(jnp.int32, (GLA_BLOCK, 2 * GLA_BLOCK), 1) & (GLA_BLOCK - 1)
                  ) <= lax.broadcasted_iota(jnp.int32, (GLA_BLOCK, 2 * GLA_BLOCK), 0)
    eye = (lax.broadcasted_iota(jnp.int32, (LANES, LANES), 0)
           == lax.broadcasted_iota(jnp.int32, (LANES, LANES), 1))
    gla_tiles = [(p, g) for g in range(n_gblk) for p in range(n_pair)]
    zero_kt = jnp.zeros((GLA_DK, GLA_BLOCK), BF16)
    attn, upd, decay_col = {}, {}, {}
    for n, (p, g) in enumerate(gla_tiles):
        fill(FILL_PER_GLA_DOT[n % len(FILL_PER_GLA_DOT)])
        cols = slice(p * LANES, (p + 1) * LANES)
        blk = slice(g * GLA_BLOCK, (g + 1) * GLA_BLOCK)
        u = _mm(k_end_t[cols, blk], v_b[blk, p * pair_w:(p + 1) * pair_w])
        upd[p, g] = jnp.concatenate([u[:GLA_DK, :GLA_DV], u[GLA_DK:, GLA_DV:]], axis=0)
    for n, (p, g) in enumerate(gla_tiles):
        fill(FILL_PER_GLA_DOT[n % len(FILL_PER_GLA_DOT)])
        cols = slice(p * LANES, (p + 1) * LANES)
        blk = slice(g * GLA_BLOCK, (g + 1) * GLA_BLOCK)
        ka2 = k_att_t[cols, blk]
        k_diag = jnp.concatenate([jnp.concatenate([ka2[:GLA_DK], zero_kt], axis=1),
                                  jnp.concatenate([zero_kt, ka2[GLA_DK:]], axis=1)], axis=0)
        attn[p, g] = jnp.where(causal_blk, _mm(q_att[blk, cols], k_diag), 0.0).astype(BF16)
        last = (g + 1) * GLA_BLOCK - 1
        total = l1[last:last + 1, cols] + l2[last:last + 1, cols]
        decay_row = jnp.broadcast_to(jnp.exp(total), (LANES, LANES))
        decay_col[p, g] = jnp.sum(jnp.where(eye, decay_row, 0.0), axis=1, keepdims=True)

    states = {}
    for p in range(n_pair):
        state = s_ref[p * LANES:(p + 1) * LANES, :]
        for g in range(n_gblk):
            states[p, g] = state.astype(BF16)
            state = state * decay_col[p, g] + upd[p, g]
        s_ref[p * LANES:(p + 1) * LANES, :] = state

    o_swa, o_gla = {}, {}
    zero_vb = jnp.zeros((GLA_BLOCK, GLA_DV), BF16)
    zero_vs = jnp.zeros((GLA_DK, GLA_DV), BF16)
    norm_w = nw_ref[...]
    assert len(swa_tiles) == len(gla_tiles) and OUT_ROWS * 2 == TILE
    per_half = len(swa_tiles) // 2

    def value_dots(n):
        (h, i), (p, g) = swa_tiles[n], gla_tiles[n]
        kv_rows = slice(i * BLOCK, (i + 2) * BLOCK)
        o_swa[h, i] = _mm(v_t[h * HALF:(h + 1) * HALF, kv_rows], probs[h, i])
        blk = slice(g * GLA_BLOCK, (g + 1) * GLA_BLOCK)
        v2 = v_b[blk, p * pair_w:(p + 1) * pair_w]
        st = states[p, g]
        rhs = jnp.concatenate([
            jnp.concatenate([v2[:, :GLA_DV], zero_vb], axis=1),
            jnp.concatenate([zero_vb, v2[:, GLA_DV:]], axis=1),
            jnp.concatenate([st[:GLA_DK], zero_vs], axis=1),
            jnp.concatenate([zero_vs, st[GLA_DK:]], axis=1)], axis=0)
        lhs = jnp.concatenate([attn[p, g], q_dec[blk, p * LANES:(p + 1) * LANES]], axis=1)
        o_gla[p, g] = _mm(lhs, rhs)

    def gated_outputs(half):
        for n in range(half * per_half, (half + 1) * per_half):
            (h, i), (p, g) = swa_tiles[n], gla_tiles[n]
            rows = slice(i * BLOCK, (i + 1) * BLOCK)
            o_t, inv = o_swa[h, i], inv_den[h, i]
            for c in range(2):
                cg = 2 * h + c
                o2_t = jnp.concatenate(
                    [o_t[:, (2 * c + par) * BLOCK:(2 * c + par + 1) * BLOCK] * inv[2 * c + par]
                     for par in range(2)], axis=0)
                cg_cols = slice(cg * LANES, (cg + 1) * LANES)
                mix_ref[rows, cg_cols] = (o2_t.T * gate_a[rows, cg_cols]).astype(BF16)
            rows = slice(g * GLA_BLOCK, (g + 1) * GLA_BLOCK)
            for par in range(2):
                o = o_gla[p, g][:, par * GLA_DV:(par + 1) * GLA_DV]
                o = o * lax.rsqrt(jnp.mean(jnp.square(o), axis=-1, keepdims=True) + EPS) * norm_w
                c0 = (2 * p + par) * GLA_DV
                mix_ref[rows, SWA_WIDTH + c0:SWA_WIDTH + c0 + GLA_DV] = (
                    o * gate_b[rows, c0:c0 + GLA_DV]).astype(BF16)

    def layer_norm_store(half, y):
        rows = slice(half * OUT_ROWS, (half + 1) * OUT_ROWS)
        y = y + ALPHA * x_ref[rows, :]
        mu = jnp.mean(y, axis=-1, keepdims=True)
        yc = y - mu
        var = jnp.mean(jnp.square(yc), axis=-1, keepdims=True)
        o_ref[rows, :] = yc * lax.rsqrt(var + EPS) * lng_ref[...] + lnb_ref[...]

    fill(len(pending))
    gate_a = jax.nn.silu(filled(GA))
    gate_b = jax.nn.silu(filled(GB))
    for n in range(per_half):
        value_dots(n)
    gated_outputs(0)
    out_piece = 2 * LANES
    n_piece = D_MODEL // out_piece
    y0 = []
    for n in range(per_half, 2 * per_half):
        value_dots(n)
        if n - per_half < n_piece:
            k = n - per_half
            y0.append(_mm(mix_ref[:OUT_ROWS, :], w_out_ref[:, k * out_piece:(k + 1) * out_piece]))
    layer_norm_store(0, jnp.concatenate(y0, axis=1))
    gated_outputs(1)
    layer_norm_store(1, _mm(mix_ref[OUT_ROWS:, :], w_out_ref[...]))


def _rope_freq_rows():
    half = ROT_DIM // 2
    inv_freq = ROPE_THETA ** (-jnp.arange(half, dtype=F32) / half)
    return jnp.broadcast_to(inv_freq[:, None], (half, BLOCK))


@jax.jit
def kernel(x, positions, w_in, gla_w_gate_up, gla_b_gate, attn_sinks, gla_norm_w, w_out, ln_g, ln_b):
    B, S, D = x.shape
    assert D == D_MODEL and S % TILE == 0 and w_in.shape == (DEPTH, D_MODEL, D_IN_PROJ)
    n_tiles = S // TILE
    pos_rows = positions.reshape(B * n_tiles, 1, TILE)

    def const(shape):
        return pl.BlockSpec((None,) + shape, lambda b, t: (0,) * (len(shape) + 1),
                            pipeline_mode=pl.Buffered(1))

    grid_spec = pl.GridSpec(
        grid=(B, n_tiles),
        in_specs=[
            pl.BlockSpec((None, TILE, D_MODEL), lambda b, t: (b, t, 0)),
            pl.BlockSpec((None, 1, TILE), lambda b, t: (b * n_tiles + t, 0, 0)),
            const((D_IN_PROJ, D_MODEL)),
            const((GLA_RANK, GLA_HEADS * GLA_DK)),
            const((1, GLA_HEADS * GLA_DK)),
            const((1, GLA_DV)),
            const((D_MODEL, D_MODEL)),
            const((1, D_MODEL)),
            const((1, D_MODEL)),
            pl.BlockSpec((ROT_DIM // 2, BLOCK), lambda b, t: (0, 0)),
            pl.BlockSpec(memory_space=pltpu.SMEM),
        ],
        out_specs=pl.BlockSpec((None, TILE, D_MODEL), lambda b, t: (b, t, 0)),
        scratch_shapes=[
            pltpu.VMEM((BLOCK, LANES), F32),
            pltpu.VMEM((BLOCK, LANES), F32),
            pltpu.VMEM((GLA_HEADS * GLA_DK, GLA_DV), F32),
            pltpu.VMEM((TILE, D_MODEL), BF16),
            pltpu.VMEM((RB // LANES, D_MODEL, LANES), BF16),
            pltpu.VMEM((D_MODEL, LANES), BF16),
            pltpu.VMEM((LANES, GLA_HEADS * GLA_DK), BF16),
            pltpu.VMEM((D_MODEL, D_MODEL), BF16),
            pltpu.VMEM((CUM_ROWS, CUM_ROWS), BF16),
        ],
    )
    row = lambda a: a.reshape(DEPTH, 1, -1)
    return pl.pallas_call(
        _layer_kernel,
        grid_spec=grid_spec,
        out_shape=jax.ShapeDtypeStruct((B, S, D_MODEL), x.dtype),
        compiler_params=pltpu.CompilerParams(
            dimension_semantics=("arbitrary", "arbitrary"),
            vmem_limit_bytes=VMEM_LIMIT_BYTES),
        name="hymba_layer",
    )(x, pos_rows, jnp.swapaxes(w_in, 1, 2), gla_w_gate_up, row(gla_b_gate), row(gla_norm_w),
      w_out, row(ln_g), row(ln_b), _rope_freq_rows(), attn_sinks[0])
```

```python
import functools

import jax
import jax.numpy as jnp
from jax import lax
from jax.experimental import pallas as pl
from jax.experimental.pallas import tpu as pltpu

D_MODEL = 1024
SWA_WIDTH = 512
SWA_HEAD_DIM = 64
SWA_Q_HEADS = 8
SWA_KV_HEADS = 2
WINDOW = 128
BLOCK = 128
ROPE_THETA = 500000.0
ROT_DIM = 16
GLA_WIDTH = 512
GLA_HEADS = 4
GLA_DK = 64
GLA_DV = 128
GLA_RANK = 16
GLA_TAU = 16.0
GLA_CHUNK = 64
EPS = 1e-5
DEPTH = 1
ALPHA = (2 * DEPTH) ** 0.25

LANES = 128
HALF = LANES // 2

QA, KA, VA, GA, QB, KB, VB, GB, RB = 0, 512, 640, 768, 1280, 1536, 1792, 2304, 2816
D_IN_PROJ = RB + GLA_RANK

TILE = 1024
SUB = 512
OUT_ROWS = SUB // 2
CUM_ROWS = 256
GLA_BLOCK = 2 * GLA_CHUNK
PIECE = 2 * LANES
W_IN_ROWS = 256
W_STEPS = RB // W_IN_ROWS
W_OUT_ROWS = 128
W_OUT_STEPS = D_MODEL // W_OUT_ROWS
assert W_OUT_STEPS <= W_STEPS and TILE == 2 * SUB
VMEM_LIMIT_BYTES = 60 * 1024 * 1024

F32 = jnp.float32
BF16 = jnp.bfloat16


def _mm(a, b):
    return jnp.dot(a, b, preferred_element_type=F32)


def _layer_kernel(tiles_per_seq, x_ref, pos_ref, w_in_blk, w_rank_t, w_up_f32, bg_ref, nw_ref,
                  w_out_blk, lng_ref, lnb_ref, invf_ref, sinks_ref,
                  o_ref, kprev_ref, vprev_ref, s_ref, mix_ref,
                  w_in_ref, w_r_ref, w_up_ref, w_out_ref, tri_ref):
    step = pl.program_id(0)

    @pl.when(step < W_STEPS)
    def _cast_weight_block():
        for k in range(W_IN_ROWS // LANES):
            w_in_ref[step * (W_IN_ROWS // LANES) + k] = (
                w_in_blk[k * LANES:(k + 1) * LANES, :].T.astype(BF16))

    @pl.when(step < W_OUT_STEPS)
    def _cast_w_out_block():
        rows = pl.ds(pl.multiple_of(step * W_OUT_ROWS, W_OUT_ROWS), W_OUT_ROWS)
        w_out_ref[rows, :] = w_out_blk[...].astype(BF16)

    @pl.when(step == 0)
    def _cast_small_weights():
        rank_rows = jnp.concatenate(
            [w_rank_t[...], jnp.zeros((LANES - GLA_RANK, D_MODEL), F32)], axis=0)
        w_r_ref[...] = rank_rows.T.astype(BF16)
        w_up_ref[...] = jnp.concatenate(
            [w_up_f32[...], jnp.zeros((LANES - GLA_RANK, GLA_HEADS * GLA_DK), F32)],
            axis=0).astype(BF16)
        ri = lax.broadcasted_iota(jnp.int32, (CUM_ROWS, CUM_ROWS), 0)
        ci = lax.broadcasted_iota(jnp.int32, (CUM_ROWS, CUM_ROWS), 1)
        tri_ref[...] = ((ri // GLA_CHUNK == ci // GLA_CHUNK) & (ci <= ri)).astype(BF16)

    t_idx = (step - W_STEPS) % tiles_per_seq

    @pl.when((step >= W_STEPS) & (t_idx == 0))
    def _reset_carries():
        kprev_ref[...] = jnp.zeros_like(kprev_ref)
        vprev_ref[...] = jnp.zeros_like(vprev_ref)
        s_ref[...] = jnp.zeros_like(s_ref)

    @pl.when(step >= W_STEPS)
    def _token_tile():
        _tile_body(t_idx, x_ref, pos_ref, bg_ref, nw_ref, lng_ref, lnb_ref, invf_ref, sinks_ref,
                   o_ref, kprev_ref, vprev_ref, s_ref, mix_ref,
                   w_in_ref, w_r_ref, w_up_ref, w_out_ref, tri_ref)


def _tile_body(t_idx, x_ref, pos_ref, bg_ref, nw_ref, lng_ref, lnb_ref, invf_ref, sinks_ref,
               o_ref, kprev_ref, vprev_ref, s_ref, mix_ref,
               w_in_ref, w_r_ref, w_up_ref, w_out_ref, tri_ref):
    n_blk = SUB // BLOCK
    n_gblk = SUB // GLA_BLOCK
    n_pair = GLA_HEADS // 2
    pair_w = 2 * GLA_DV
    gla_lanes = GLA_HEADS * GLA_DK
    swa_tiles = [(h, i) for i in range(n_blk) for h in range(SWA_KV_HEADS)]
    gla_tiles = [(p, g) for g in range(n_gblk) for p in range(n_pair)]
    assert len(swa_tiles) == len(gla_tiles)
    per_half = len(swa_tiles) // 2
    n_out_piece = D_MODEL // PIECE

    lane = lax.broadcasted_iota(jnp.int32, (1, LANES), 1)
    lo_half = lane < HALF
    kj = lax.broadcasted_iota(jnp.int32, (2 * BLOCK, BLOCK), 0)
    qj = lax.broadcasted_iota(jnp.int32, (2 * BLOCK, BLOCK), 1)
    in_window = (kj > qj) & (kj <= qj + WINDOW)
    first_blk_mask = in_window & ((kj >= BLOCK) | (t_idx > 0))
    causal_blk = (lax.broadcasted_iota(jnp.int32, (GLA_BLOCK, 2 * GLA_BLOCK), 1) & (GLA_BLOCK - 1)
                  ) <= lax.broadcasted_iota(jnp.int32, (GLA_BLOCK, 2 * GLA_BLOCK), 0)
    eye = (lax.broadcasted_iota(jnp.int32, (LANES, LANES), 0)
           == lax.broadcasted_iota(jnp.int32, (LANES, LANES), 1))
    zero_kt = jnp.zeros((GLA_DK, GLA_BLOCK), BF16)
    zero_vb = jnp.zeros((GLA_BLOCK, GLA_DV), BF16)
    zero_vs = jnp.zeros((GLA_DK, GLA_DV), BF16)
    norm_w = nw_ref[...]

    subs = [dict(), dict()]

    def rows_of(j, lo=0, n=SUB):
        return slice(j * SUB + lo, j * SUB + lo + n)

    def load_x(j):
        subs[j]["xb"] = x_ref[rows_of(j), :].astype(BF16)

    def piece(j, lo, c):
        def run():
            groups = [w_in_ref[g] for g in range((lo + c) // LANES, (lo + c + PIECE) // LANES)]
            subs[j][lo, c] = _mm(subs[j]["xb"], jnp.concatenate(groups, axis=1))
        return run

    def wide(j, lo):
        return jnp.concatenate([subs[j][lo, 0], subs[j][lo, PIECE]], axis=1)

    def rank_proj(j):
        def run():
            subs[j]["r_b"] = _mm(subs[j]["xb"], w_r_ref[...]).astype(BF16)
        return run

    def decay_logits(j):
        d = subs[j]
        logit = _mm(d["r_b"], w_up_ref[...]) + bg_ref[...]
        d["log_a"] = jax.nn.log_sigmoid(logit) / GLA_TAU

    def decay_cumsum(j):
        d = subs[j]
        tri = tri_ref[...]
        la_hi = d["log_a"].astype(BF16)
        la_lo = (d["log_a"] - la_hi.astype(F32)).astype(BF16)
        d["b"] = jnp.concatenate(
            [_mm(tri, la_hi[r:r + CUM_ROWS]) + _mm(tri, la_lo[r:r + CUM_ROWS])
             for r in range(0, SUB, CUM_ROWS)], axis=0)

    def rotary(j, k_prev, v_prev):
        d = subs[j]
        half = ROT_DIM // 2
        pos = pos_ref[:, rows_of(j)].astype(F32)
        ones_r = jnp.ones((HALF - ROT_DIM, BLOCK), F32)
        zeros_r = jnp.zeros((HALF - ROT_DIM, BLOCK), F32)
        zeros_h = jnp.zeros((half, BLOCK), F32)
        cos_blk, up_blk, dn_blk = [], [], []
        for i in range(n_blk):
            ang = invf_ref[...] * pos[:, i * BLOCK:(i + 1) * BLOCK]
            c8, s8 = jnp.cos(ang), jnp.sin(ang)
            cos_blk.append(jnp.concatenate([c8, c8, ones_r] * 2, axis=0).T)
            up_blk.append(jnp.concatenate([zeros_h, s8, zeros_r] * 2, axis=0).T)
            dn_blk.append(jnp.concatenate([-s8, zeros_h, zeros_r] * 2, axis=0).T)
        cos = jnp.concatenate(cos_blk, axis=0)
        sin_up = jnp.concatenate(up_blk, axis=0)
        sin_dn = jnp.concatenate(dn_blk, axis=0)

        def rope(t):
            return (t * cos + pltpu.roll(t, ROT_DIM // 2, 1) * sin_up
                    + pltpu.roll(t, LANES - ROT_DIM // 2, 1) * sin_dn)

        kv = d[KA, 0]
        k_new = rope(kv[:, :LANES])
        v_new = kv[:, LANES:]
        k_cat = jnp.concatenate([k_prev, k_new], axis=0)
        v_cat = jnp.concatenate([v_prev, v_new], axis=0)
        d["k_last"] = k_new[SUB - BLOCK:]
        d["v_last"] = v_new[SUB - BLOCK:]
        k_sw = pltpu.roll(k_cat, HALF, 1)
        k_cb, k_sb = k_cat.astype(BF16), k_sw.astype(BF16)
        k_zero = jnp.zeros_like(k_cb)
        d["k_even"] = [jnp.where(lo_half, k_cb, k_zero), jnp.where(lo_half, k_sb, k_zero)]
        d["k_odd"] = [jnp.where(lo_half, k_zero, k_sb), jnp.where(lo_half, k_zero, k_cb)]
        d["v_t"] = v_cat.T.astype(BF16)
        scale = SWA_HEAD_DIM ** -0.5
        q_a = wide(j, QA)
        d["q_t"] = [(rope(q_a[:, cg * LANES:(cg + 1) * LANES]) * scale).T.astype(BF16)
                    for cg in range(SWA_WIDTH // LANES)]
        d["scores"] = {}

    def score_dots(j, n):
        def run():
            d = subs[j]
            h, i = swa_tiles[n]
            rows = slice(i * BLOCK, (i + 1) * BLOCK)
            kv_rows = slice(i * BLOCK, (i + 2) * BLOCK)
            q_pair = jnp.concatenate([d["q_t"][2 * h][:, rows], d["q_t"][2 * h + 1][:, rows]], axis=1)
            d["scores"][h, i] = [_mm(d["k_even"][h][kv_rows], q_pair),
                                 _mm(d["k_odd"][h][kv_rows], q_pair)]
        return run

    def softmax(j):
        d = subs[j]
        probs, inv_den = {}, {}
        for h, i in swa_tiles:
            mask = first_blk_mask if (j == 0 and i == 0) else in_window
            p_t, inv = [], []
            for g in range(4):
                sink = sinks_ref[4 * h + g]
                s = d["scores"][h, i][g % 2][:, (g // 2) * BLOCK:(g // 2 + 1) * BLOCK]
                s = jnp.where(mask, s, -jnp.inf)
                m = jnp.maximum(jnp.max(s, axis=0, keepdims=True), sink)
                e = jnp.exp(s - m)
                denom = jnp.sum(e, axis=0, keepdims=True) + jnp.exp(sink - m)
                inv.append(1.0 / denom)
                p_t.append(e.astype(BF16))
            probs[h, i] = jnp.concatenate(p_t, axis=1)
            inv_den[h, i] = inv
        d["probs"], d["inv_den"] = probs, inv_den

    def gla_decays(j):
        d = subs[j]
        b = d["b"]
        q_b, k_b = d[QB, 0], d[QB, PIECE]

        def chunk_end_rows(c):
            return jnp.broadcast_to(b[(c + 1) * GLA_CHUNK - 1:(c + 1) * GLA_CHUNK, :], (GLA_CHUNK, gla_lanes))

        zeros_c = jnp.zeros((GLA_CHUNK, gla_lanes), F32)
        l1 = jnp.concatenate([chunk_end_rows(2 * g) for g in range(n_gblk) for _ in range(2)], axis=0)
        l1_c1 = jnp.concatenate([blk for g in range(n_gblk) for blk in (chunk_end_rows(2 * g), zeros_c)], axis=0)
        l1_c2 = l1 - l1_c1
        l2 = jnp.concatenate([chunk_end_rows(2 * g + 1) for g in range(n_gblk) for _ in range(2)], axis=0)
        pivot = b - l1_c1
        q_s = q_b * (GLA_DK ** -0.5)
        d["q_att"] = (q_s * jnp.exp(pivot)).astype(BF16)
        d["k_att_t"] = (k_b * jnp.exp(-pivot)).T.astype(BF16)
        d["q_dec"] = (q_s * jnp.exp(b + l1_c2)).astype(BF16)
        d["k_end_t"] = (k_b * jnp.exp(l1_c1 - b + l2)).T.astype(BF16)
        d["l12"] = l1 + l2
        d["v_b"] = wide(j, VB).astype(BF16)
        d["attn"], d["upd"], d["decay_col"] = {}, {}, {}

    def update_dot(j, n):
        def run():
            d = subs[j]
            p, g = gla_tiles[n]
            cols = slice(p * LANES, (p + 1) * LANES)
            blk = slice(g * GLA_BLOCK, (g + 1) * GLA_BLOCK)
            u = _mm(d["k_end_t"][cols, blk], d["v_b"][blk, p * pair_w:(p + 1) * pair_w])
            d["upd"][p, g] = jnp.concatenate([u[:GLA_DK, :GLA_DV], u[GLA_DK:, GLA_DV:]], axis=0)
        return run

    def intra_dot(j, n):
        def run():
            d = subs[j]
            p, g = gla_tiles[n]
            cols = slice(p * LANES, (p + 1) * LANES)
            blk = slice(g * GLA_BLOCK, (g + 1) * GLA_BLOCK)
            ka2 = d["k_att_t"][cols, blk]
            k_diag = jnp.concatenate([jnp.concatenate([ka2[:GLA_DK], zero_kt], axis=1),
                                      jnp.concatenate([zero_kt, ka2[GLA_DK:]], axis=1)], axis=0)
            d["attn"][p, g] = jnp.where(causal_blk, _mm(d["q_att"][blk, cols], k_diag), 0.0).astype(BF16)
            last = (g + 1) * GLA_BLOCK - 1
            decay_row = jnp.broadcast_to(jnp.exp(d["l12"][last:last + 1, cols]), (LANES, LANES))
            d["decay_col"][p, g] = jnp.sum(jnp.where(eye, decay_row, 0.0), axis=1, keepdims=True)
        return run

    def recurrence(j, state):
        d = subs[j]
        states, new_state = {}, []
        for p in range(n_pair):
            st = state[p]
            for g in range(n_gblk):
                states[p, g] = st.astype(BF16)
                st = st * d["decay_col"][p, g] + d["upd"][p, g]
            new_state.append(st)
        d["states"] = states
        d["o_swa"], d["o_gla"] = {}, {}
        return new_state

    def value_dots(j, n):
        def run():
            d = subs[j]
            (h, i), (p, g) = swa_tiles[n], gla_tiles[n]
            kv_rows = slice(i * BLOCK, (i + 2) * BLOCK)
            d["o_swa"][h, i] = _mm(d["v_t"][h * HALF:(h + 1) * HALF, kv_rows], d["probs"][h, i])
            blk = slice(g * GLA_BLOCK, (g + 1) * GLA_BLOCK)
            v2 = d["v_b"][blk, p * pair_w:(p + 1) * pair_w]
            st = d["states"][p, g]
            rhs = jnp.concatenate([
                jnp.concatenate([v2[:, :GLA_DV], zero_vb], axis=1),
                jnp.concatenate([zero_vb, v2[:, GLA_DV:]], axis=1),
                jnp.concatenate([st[:GLA_DK], zero_vs], axis=1),
                jnp.concatenate([zero_vs, st[GLA_DK:]], axis=1)], axis=0)
            lhs = jnp.concatenate([d["attn"][p, g], d["q_dec"][blk, p * LANES:(p + 1) * LANES]], axis=1)
            d["o_gla"][p, g] = _mm(lhs, rhs)
        return run

    def gated_outputs(j, half):
        d = subs[j]
        if "gate_a" not in d:
            d["gate_a"] = jax.nn.silu(wide(j, GA))
            d["gate_b"] = jax.nn.silu(wide(j, GB))
        for n in range(half * per_half, (half + 1) * per_half):
            (h, i), (p, g) = swa_tiles[n], gla_tiles[n]
            rows = slice(i * BLOCK, (i + 1) * BLOCK)
            o_t, inv = d["o_swa"][h, i], d["inv_den"][h, i]
            for c in range(2):
                cg = 2 * h + c
                o2_t = jnp.concatenate(
                    [o_t[:, (2 * c + par) * BLOCK:(2 * c + par + 1) * BLOCK] * inv[2 * c + par]
                     for par in range(2)], axis=0)
                cg_cols = slice(cg * LANES, (cg + 1) * LANES)
                mix_ref[rows_of(j, i * BLOCK, BLOCK), cg_cols] = (
                    o2_t.T * d["gate_a"][rows, cg_cols]).astype(BF16)
            rows = slice(g * GLA_BLOCK, (g + 1) * GLA_BLOCK)
            for par in range(2):
                o = d["o_gla"][p, g][:, par * GLA_DV:(par + 1) * GLA_DV]
                o = o * lax.rsqrt(jnp.mean(jnp.square(o), axis=-1, keepdims=True) + EPS) * norm_w
                c0 = (2 * p + par) * GLA_DV
                mix_ref[rows_of(j, g * GLA_BLOCK, GLA_BLOCK), SWA_WIDTH + c0:SWA_WIDTH + c0 + GLA_DV] = (
                    o * d["gate_b"][rows, c0:c0 + GLA_DV]).astype(BF16)

    def out_piece(j, half, k):
        def run():
            rows = rows_of(j, half * OUT_ROWS, OUT_ROWS)
            subs[j]["y", half, k] = _mm(mix_ref[rows, :], w_out_ref[:, k * PIECE:(k + 1) * PIECE])
        return run

    def layer_norm_store(j, half):
        rows = rows_of(j, half * OUT_ROWS, OUT_ROWS)
        y = jnp.concatenate([subs[j]["y", half, k] for k in range(n_out_piece)], axis=1)
        y = y + ALPHA * x_ref[rows, :]
        mu = jnp.mean(y, axis=-1, keepdims=True)
        yc = y - mu
        var = jnp.mean(jnp.square(yc), axis=-1, keepdims=True)
        o_ref[rows, :] = yc * lax.rsqrt(var + EPS) * lng_ref[...] + lnb_ref[...]

    def interleave(smalls, bigs, every):
        bigs = list(bigs)
        for n, small in enumerate(smalls):
            if n % every == 0 and bigs:
                bigs.pop(0)()
            small()
        for big in bigs:
            big()

    n_t = len(swa_tiles)
    A, B = 0, 1
    load_x(A)
    for run in (rank_proj(A), piece(A, KA, 0), piece(A, QA, 0), piece(A, QA, PIECE)):
        run()
    decay_logits(A)
    for run in (piece(A, QB, 0), piece(A, QB, PIECE)):
        run()
    decay_cumsum(A)
    piece(A, GB, 0)()
    rotary(A, kprev_ref[...], vprev_ref[...])
    load_x(B)
    interleave([score_dots(A, n) for n in range(n_t)],
               [piece(A, VB, 0), piece(A, VB, PIECE), rank_proj(B), piece(B, KA, 0)], every=2)
    softmax(A)
    gla_decays(A)
    interleave([update_dot(A, n) for n in range(n_t)], [piece(B, QA, 0), piece(B, QA, PIECE)], every=4)
    decay_logits(B)
    interleave([intra_dot(A, n) for n in range(n_t)], [piece(B, QB, 0), piece(B, QB, PIECE)], every=4)
    decay_cumsum(B)
    rotary(B, subs[A]["k_last"], subs[A]["v_last"])
    state = recurrence(A, [s_ref[p * LANES:(p + 1) * LANES, :] for p in range(n_pair)])
    interleave([value_dots(A, n) for n in range(per_half)],
               [piece(A, GA, 0), piece(A, GA, PIECE), piece(A, GB, PIECE)], every=1)
    gated_outputs(A, 0)
    interleave([value_dots(A, n) for n in range(per_half, n_t)],
               [out_piece(A, 0, k) for k in range(n_out_piece)], every=1)
    layer_norm_store(A, 0)
    gated_outputs(A, 1)
    piece(B, GB, 0)()
    interleave([score_dots(B, n) for n in range(n_t)],
               [piece(B, VB, 0), piece(B, VB, PIECE), out_piece(A, 1, 0), out_piece(A, 1, 1)], every=2)
    softmax(B)
    gla_decays(B)
    interleave([update_dot(B, n) for n in range(n_t)], [out_piece(A, 1, 2), out_piece(A, 1, 3)], every=4)
    layer_norm_store(A, 1)
    interleave([intra_dot(B, n) for n in range(n_t)], [piece(B, GA, 0), piece(B, GA, PIECE)], every=4)
    state = recurrence(B, state)
    interleave([value_dots(B, n) for n in range(per_half)], [piece(B, GB, PIECE)], every=1)
    gated_outputs(B, 0)
    interleave([value_dots(B, n) for n in range(per_half, n_t)],
               [out_piece(B, 0, k) for k in range(n_out_piece)], every=1)
    layer_norm_store(B, 0)
    gated_outputs(B, 1)
    for k in range(n_out_piece):
        out_piece(B, 1, k)()
    layer_norm_store(B, 1)
    kprev_ref[...] = subs[B]["k_last"]
    vprev_ref[...] = subs[B]["v_last"]
    for p in range(n_pair):
        s_ref[p * LANES:(p + 1) * LANES, :] = state[p]


def _rope_freq_rows():
    half = ROT_DIM // 2
    inv_freq = ROPE_THETA ** (-jnp.arange(half, dtype=F32) / half)
    return jnp.broadcast_to(inv_freq[:, None], (half, BLOCK))


@jax.jit
def kernel(x, positions, w_in, gla_w_gate_up, gla_b_gate, attn_sinks, gla_norm_w, w_out, ln_g, ln_b):
    B, S, D = x.shape
    assert D == D_MODEL and S % TILE == 0 and w_in.shape == (DEPTH, D_MODEL, D_IN_PROJ)
    tiles_per_seq = S // TILE
    n_tiles = B * tiles_per_seq
    x_tiles = x.reshape(n_tiles, TILE, D_MODEL)
    pos_rows = positions.reshape(n_tiles, 1, TILE)
    w_in_t = jnp.swapaxes(w_in, 1, 2)

    def const(shape):
        return pl.BlockSpec((None,) + shape, lambda s: (0,) * (len(shape) + 1),
                            pipeline_mode=pl.Buffered(1))

    tile = lambda s: (jnp.maximum(s - W_STEPS, 0), 0, 0)
    grid_spec = pl.GridSpec(
        grid=(W_STEPS + n_tiles,),
        in_specs=[
            pl.BlockSpec((None, TILE, D_MODEL), tile),
            pl.BlockSpec((None, 1, TILE), tile),
            pl.BlockSpec((None, W_IN_ROWS, D_MODEL), lambda s: (0, jnp.minimum(s, W_STEPS - 1), 0)),
            const((GLA_RANK, D_MODEL)),
            const((GLA_RANK, GLA_HEADS * GLA_DK)),
            const((1, GLA_HEADS * GLA_DK)),
            const((1, GLA_DV)),
            pl.BlockSpec((None, W_OUT_ROWS, D_MODEL), lambda s: (0, jnp.minimum(s, W_OUT_STEPS - 1), 0)),
            const((1, D_MODEL)),
            const((1, D_MODEL)),
            pl.BlockSpec((ROT_DIM // 2, BLOCK), lambda s: (0, 0)),
            pl.BlockSpec(memory_space=pltpu.SMEM),
        ],
        out_specs=pl.BlockSpec((None, TILE, D_MODEL), tile),
        scratch_shapes=[
            pltpu.VMEM((BLOCK, LANES), F32),
            pltpu.VMEM((BLOCK, LANES), F32),
            pltpu.VMEM((GLA_HEADS * GLA_DK, GLA_DV), F32),
            pltpu.VMEM((TILE, D_MODEL), BF16),
            pltpu.VMEM((RB // LANES, D_MODEL, LANES), BF16),
            pltpu.VMEM((D_MODEL, LANES), BF16),
            pltpu.VMEM((LANES, GLA_HEADS * GLA_DK), BF16),
            pltpu.VMEM((D_MODEL, D_MODEL), BF16),
            pltpu.VMEM((CUM_ROWS, CUM_ROWS), BF16),
        ],
    )
    row = lambda a: a.reshape(DEPTH, 1, -1)
    out = pl.pallas_call(
        functools.partial(_layer_kernel, tiles_per_seq),
        grid_spec=grid_spec,
        out_shape=jax.ShapeDtypeStruct((n_tiles, TILE, D_MODEL), x.dtype),
        compiler_params=pltpu.CompilerParams(
            dimension_semantics=("arbitrary",),
            vmem_limit_bytes=VMEM_LIMIT_BYTES),
        name="hymba_layer",
    )(x_tiles, pos_rows, w_in_t, w_in_t[:, RB:, :], gla_w_gate_up, row(gla_b_gate),
      row(gla_norm_w), w_out, row(ln_g), row(ln_b), _rope_freq_rows(), attn_sinks[0])
    return out.reshape(B, S, D_MODEL)
```

```python
import jax
import jax.numpy as jnp
from jax import lax
from jax.experimental import pallas as pl
from jax.experimental.pallas import tpu as pltpu

D_MODEL = 1024
SWA_WIDTH = 512
SWA_HEAD_DIM = 64
SWA_Q_HEADS = 8
SWA_KV_HEADS = 2
WINDOW = 128
BLOCK = 128
ROPE_THETA = 500000.0
ROT_DIM = 16
GLA_WIDTH = 512
GLA_HEADS = 4
GLA_DK = 64
GLA_DV = 128
GLA_RANK = 16
GLA_TAU = 16.0
GLA_CHUNK = 64
EPS = 1e-5
DEPTH = 1
ALPHA = (2 * DEPTH) ** 0.25

LANES = 128
HALF = LANES // 2

QA, KA, VA, GA, QB, KB, VB, GB, RB = 0, 512, 640, 768, 1280, 1536, 1792, 2304, 2816
D_IN_PROJ = RB + GLA_RANK

TILE = 512
CUM_ROWS = 256
OUT_ROWS = 256
GLA_BLOCK = 2 * GLA_CHUNK
FILL_ROWS = TILE
FILL_ORDER = ((GB, 0), (VB, 0), (VB, 256), (GA, 0), (GA, 256), (GB, 256))
FILL_BEFORE_SCORES = 1
FILL_PER_SCORE_TILE = (1, 0, 0, 0)
FILL_PER_GLA_DOT = (1, 0, 0, 0, 0, 0, 0, 0)
W_CAST_ROWS = 128
VMEM_LIMIT_BYTES = 60 * 1024 * 1024

F32 = jnp.float32
BF16 = jnp.bfloat16


def _mm(a, b):
    return jnp.dot(a, b, preferred_element_type=F32)


def _layer_kernel(x_ref, pos_ref, w_in_t_f32, w_up_f32, bg_ref, nw_ref, w_out_f32,
                  lng_ref, lnb_ref, invf_ref, sinks_ref,
                  o_ref, kprev_ref, vprev_ref, s_ref, mix_ref,
                  w_in_ref, w_r_ref, w_up_ref, w_out_ref, tri_ref):
    t_idx = pl.program_id(1)

    @pl.when((pl.program_id(0) == 0) & (t_idx == 0))
    def _cast_weights():
        def cast_group(g, carry):
            rows = pl.ds(pl.multiple_of(g * LANES, LANES), LANES)
            w_in_ref[g] = w_in_t_f32[rows, :].T.astype(BF16)
            return carry
        lax.fori_loop(0, RB // LANES, cast_group, 0)
        rank_rows = jnp.concatenate(
            [w_in_t_f32[RB:, :], jnp.zeros((LANES - GLA_RANK, D_MODEL), F32)], axis=0)
        w_r_ref[...] = rank_rows.T.astype(BF16)
        ri = lax.broadcasted_iota(jnp.int32, (CUM_ROWS, CUM_ROWS), 0)
        ci = lax.broadcasted_iota(jnp.int32, (CUM_ROWS, CUM_ROWS), 1)
        tri_ref[...] = ((ri // GLA_CHUNK == ci // GLA_CHUNK) & (ci <= ri)).astype(BF16)

        def cast_rows(r, carry):
            rows = pl.ds(pl.multiple_of(r * W_CAST_ROWS, W_CAST_ROWS), W_CAST_ROWS)
            w_out_ref[rows, :] = w_out_f32[rows, :].astype(BF16)
            return carry
        lax.fori_loop(0, D_MODEL // W_CAST_ROWS, cast_rows, 0)
        w_up_ref[...] = jnp.concatenate(
            [w_up_f32[...], jnp.zeros((LANES - GLA_RANK, GLA_HEADS * GLA_DK), F32)],
            axis=0).astype(BF16)

    @pl.when(t_idx == 0)
    def _reset_carries():
        kprev_ref[...] = jnp.zeros_like(kprev_ref)
        vprev_ref[...] = jnp.zeros_like(vprev_ref)
        s_ref[...] = jnp.zeros_like(s_ref)

    xb = x_ref[...].astype(BF16)

    def proj(lo, width, rows=slice(None)):
        groups = [w_in_ref[g] for g in range(lo // LANES, (lo + width) // LANES)]
        return _mm(xb[rows], jnp.concatenate(groups, axis=1))

    fill_piece = 2 * LANES
    pending = [(lo, c, r) for lo, c in FILL_ORDER for r in range(0, TILE, FILL_ROWS)]
    pieces = {}

    def fill(n=1):
        for _ in range(n):
            if pending:
                lo, c, r = pending.pop(0)
                pieces[lo, c, r] = proj(lo + c, fill_piece, slice(r, r + FILL_ROWS))

    def filled(lo):
        return jnp.concatenate(
            [jnp.concatenate([pieces[lo, c, r] for c in range(0, GLA_WIDTH, fill_piece)], axis=1)
             for r in range(0, TILE, FILL_ROWS)], axis=0)

    lane = lax.broadcasted_iota(jnp.int32, (1, LANES), 1)
    lo_half = lane < HALF
    n_blk = TILE // BLOCK
    n_gblk = TILE // GLA_BLOCK
    n_pair = GLA_HEADS // 2
    pair_w = 2 * GLA_DV
    gla_lanes = GLA_HEADS * GLA_DK

    r_b = _mm(xb, w_r_ref[...]).astype(BF16)
    kv_new = proj(KA, 2 * LANES)
    q_a = proj(QA, SWA_WIDTH)
    logit = _mm(r_b, w_up_ref[...]) + bg_ref[...]
    log_a = jax.nn.log_sigmoid(logit) / GLA_TAU
    qk_b = proj(QB, 2 * gla_lanes)
    q_b = qk_b[:, :gla_lanes]
    k_b = qk_b[:, gla_lanes:]

    tri = tri_ref[...]
    la_hi = log_a.astype(BF16)
    la_lo = (log_a - la_hi.astype(F32)).astype(BF16)
    b = jnp.concatenate(
        [_mm(tri, la_hi[r:r + CUM_ROWS]) + _mm(tri, la_lo[r:r + CUM_ROWS])
         for r in range(0, TILE, CUM_ROWS)], axis=0)

    half = ROT_DIM // 2
    pos = pos_ref[...].astype(F32)
    ones_r = jnp.ones((HALF - ROT_DIM, BLOCK), F32)
    zeros_r = jnp.zeros((HALF - ROT_DIM, BLOCK), F32)
    zeros_h = jnp.zeros((half, BLOCK), F32)
    cos_blk, up_blk, dn_blk = [], [], []
    for i in range(TILE // BLOCK):
        ang = invf_ref[...] * pos[:, i * BLOCK:(i + 1) * BLOCK]
        c8, s8 = jnp.cos(ang), jnp.sin(ang)
        cos_blk.append(jnp.concatenate([c8, c8, ones_r] * 2, axis=0).T)
        up_blk.append(jnp.concatenate([zeros_h, s8, zeros_r] * 2, axis=0).T)
        dn_blk.append(jnp.concatenate([-s8, zeros_h, zeros_r] * 2, axis=0).T)
    cos = jnp.concatenate(cos_blk, axis=0)
    sin_up = jnp.concatenate(up_blk, axis=0)
    sin_dn = jnp.concatenate(dn_blk, axis=0)

    def rope(t):
        return (t * cos + pltpu.roll(t, ROT_DIM // 2, 1) * sin_up
                + pltpu.roll(t, LANES - ROT_DIM // 2, 1) * sin_dn)

    k_new = rope(kv_new[:, :LANES])
    v_new = kv_new[:, LANES:]
    k_cat = jnp.concatenate([kprev_ref[...], k_new], axis=0)
    v_cat = jnp.concatenate([vprev_ref[...], v_new], axis=0)
    kprev_ref[...] = k_new[TILE - BLOCK:]
    vprev_ref[...] = v_new[TILE - BLOCK:]

    k_sw = pltpu.roll(k_cat, HALF, 1)
    k_cb, k_sb = k_cat.astype(BF16), k_sw.astype(BF16)
    k_zero = jnp.zeros_like(k_cb)
    k_even = [jnp.where(lo_half, k_cb, k_zero), jnp.where(lo_half, k_sb, k_zero)]
    k_odd = [jnp.where(lo_half, k_zero, k_sb), jnp.where(lo_half, k_zero, k_cb)]
    v_t = v_cat.T.astype(BF16)

    scale = SWA_HEAD_DIM ** -0.5
    q_t = [(rope(q_a[:, cg * LANES:(cg + 1) * LANES]) * scale).T.astype(BF16)
           for cg in range(SWA_WIDTH // LANES)]

    swa_tiles = [(h, i) for i in range(n_blk) for h in range(SWA_KV_HEADS)]
    scores = {}
    fill(FILL_BEFORE_SCORES)
    for n, (h, i) in enumerate(swa_tiles):
        fill(FILL_PER_SCORE_TILE[n % len(FILL_PER_SCORE_TILE)])
        rows = slice(i * BLOCK, (i + 1) * BLOCK)
        kv_rows = slice(i * BLOCK, (i + 2) * BLOCK)
        q_pair = jnp.concatenate([q_t[2 * h][:, rows], q_t[2 * h + 1][:, rows]], axis=1)
        scores[h, i] = [_mm(k_even[h][kv_rows], q_pair),
                        _mm(k_odd[h][kv_rows], q_pair)]
    v_b = filled(VB).astype(BF16)

    kj = lax.broadcasted_iota(jnp.int32, (2 * BLOCK, BLOCK), 0)
    qj = lax.broadcasted_iota(jnp.int32, (2 * BLOCK, BLOCK), 1)
    in_window = (kj > qj) & (kj <= qj + WINDOW)
    first_blk_mask = in_window & ((kj >= BLOCK) | (t_idx > 0))
    probs, inv_den = {}, {}
    for h, i in swa_tiles:
        mask = first_blk_mask if i == 0 else in_window
        p_t, inv = [], []
        for g in range(4):
            sink = sinks_ref[4 * h + g]
            s = scores[h, i][g % 2][:, (g // 2) * BLOCK:(g // 2 + 1) * BLOCK]
            s = jnp.where(mask, s, -jnp.inf)
            m = jnp.maximum(jnp.max(s, axis=0, keepdims=True), sink)
            e = jnp.exp(s - m)
            denom = jnp.sum(e, axis=0, keepdims=True) + jnp.exp(sink - m)
            inv.append(1.0 / denom)
            p_t.append(e.astype(BF16))
        probs[h, i] = jnp.concatenate(p_t, axis=1)
        inv_den[h, i] = inv

    def chunk_end_rows(c):
        return jnp.broadcast_to(b[(c + 1) * GLA_CHUNK - 1:(c + 1) * GLA_CHUNK, :], (GLA_CHUNK, gla_lanes))

    zeros_c = jnp.zeros((GLA_CHUNK, gla_lanes), F32)
    l1 = jnp.concatenate([chunk_end_rows(2 * g) for g in range(n_gblk) for _ in range(2)], axis=0)
    l1_c1 = jnp.concatenate([blk for g in range(n_gblk) for blk in (chunk_end_rows(2 * g), zeros_c)], axis=0)
    l1_c2 = l1 - l1_c1
    l2 = jnp.concatenate([chunk_end_rows(2 * g + 1) for g in range(n_gblk) for _ in range(2)], axis=0)
    pivot = b - l1_c1
    q_s = q_b * (GLA_DK ** -0.5)
    q_att = (q_s * jnp.exp(pivot)).astype(BF16)
    k_att_t = (k_b * jnp.exp(-pivot)).T.astype(BF16)
    q_dec = (q_s * jnp.exp(b + l1_c2)).astype(BF16)
    k_end_t = (k_b * jnp.exp(l1_c1 - b + l2)).T.astype(BF16)

    causal_blk = (lax.broadcasted_iota(jnp.int32, (GLA_BLOCK, 2 * GLA_BLOCK), 1) & (GLA_BLOCK - 1)
                  ) <= lax.broadcasted_iota(jnp.int32, (GLA_BLOCK, 2 * GLA_BLOCK), 0)
    eye = (lax.broadcasted_iota(jnp.int32, (LANES, LANES), 0)
           == lax.broadcasted_iota(jnp.int32, (LANES, LANES), 1))
    gla_tiles = [(p, g) for g in range(n_gblk) for p in range(n_pair)]
    zero_kt = jnp.zeros((GLA_DK, GLA_BLOCK), BF16)
    attn, upd, decay_col = {}, {}, {}
    for n, (p, g) in enumerate(gla_tiles):
        fill(FILL_PER_GLA_DOT[n % len(FILL_PER_GLA_DOT)])
        cols = slice(p * LANES, (p + 1) * LANES)
        blk = slice(g * GLA_BLOCK, (g + 1) * GLA_BLOCK)
        u = _mm(k_end_t[cols, blk], v_b[blk, p * pair_w:(p + 1) * pair_w])
        upd[p, g] = jnp.concatenate([u[:GLA_DK, :GLA_DV], u[GLA_DK:, GLA_DV:]], axis=0)
    for n, (p, g) in enumerate(gla_tiles):
        fill(FILL_PER_GLA_DOT[n % len(FILL_PER_GLA_DOT)])
        cols = slice(p * LANES, (p + 1) * LANES)
        blk = slice(g * GLA_BLOCK, (g + 1) * GLA_BLOCK)
        ka2 = k_att_t[cols, blk]
        k_diag = jnp.concatenate([jnp.concatenate([ka2[:GLA_DK], zero_kt], axis=1),
                                  jnp.concatenate([zero_kt, ka2[GLA_DK:]], axis=1)], axis=0)
        attn[p, g] = jnp.where(causal_blk, _mm(q_att[blk, cols], k_diag), 0.0).astype(BF16)
        last = (g + 1) * GLA_BLOCK - 1
        total = l1[last:last + 1, cols] + l2[last:last + 1, cols]
        decay_row = jnp.broadcast_to(jnp.exp(total), (LANES, LANES))
        decay_col[p, g] = jnp.sum(jnp.where(eye, decay_row, 0.0), axis=1, keepdims=True)

    states = {}
    for p in range(n_pair):
        state = s_ref[p * LANES:(p + 1) * LANES, :]
        for g in range(n_gblk):
            states[p, g] = state.astype(BF16)
            state = state * decay_col[p, g] + upd[p, g]
        s_ref[p * LANES:(p + 1) * LANES, :] = state

    o_swa, o_gla = {}, {}
    zero_vb = jnp.zeros((GLA_BLOCK, GLA_DV), BF16)
    zero_vs = jnp.zeros((GLA_DK, GLA_DV), BF16)
    norm_w = nw_ref[...]
    assert len(swa_tiles) == len(gla_tiles) and OUT_ROWS * 2 == TILE
    per_half = len(swa_tiles) // 2

    def value_dots(n):
        (h, i), (p, g) = swa_tiles[n], gla_tiles[n]
        kv_rows = slice(i * BLOCK, (i + 2) * BLOCK)
        o_swa[h, i] = _mm(v_t[h * HALF:(h + 1) * HALF, kv_rows], probs[h, i])
        blk = slice(g * GLA_BLOCK, (g + 1) * GLA_BLOCK)
        v2 = v_b[blk, p * pair_w:(p + 1) * pair_w]
        st = states[p, g]
        rhs = jnp.concatenate([
            jnp.concatenate([v2[:, :GLA_DV], zero_vb], axis=1),
            jnp.concatenate([zero_vb, v2[:, GLA_DV:]], axis=1),
            jnp.concatenate([st[:GLA_DK], zero_vs], axis=1),
            jnp.concatenate([zero_vs, st[GLA_DK:]], axis=1)], axis=0)
        lhs = jnp.concatenate([attn[p, g], q_dec[blk, p * LANES:(p + 1) * LANES]], axis=1)
        o_gla[p, g] = _mm(lhs, rhs)

    def gated_outputs(half):
        for n in range(half * per_half, (half + 1) * per_half):
            (h, i), (p, g) = swa_tiles[n], gla_tiles[n]
            rows = slice(i * BLOCK, (i + 1) * BLOCK)
            o_t, inv = o_swa[h, i], inv_den[h, i]
            for c in range(2):
                cg = 2 * h + c
                o2_t = jnp.concatenate(
                    [o_t[:, (2 * c + par) * BLOCK:(2 * c + par + 1) * BLOCK] * inv[2 * c + par]
                     for par in range(2)], axis=0)
                cg_cols = slice(cg * LANES, (cg + 1) * LANES)
                mix_ref[rows, cg_cols] = (o2_t.T * gate_a[rows, cg_cols]).astype(BF16)
            rows = slice(g * GLA_BLOCK, (g + 1) * GLA_BLOCK)
            for par in range(2):
                o = o_gla[p, g][:, par * GLA_DV:(par + 1) * GLA_DV]
                o = o * lax.rsqrt(jnp.mean(jnp.square(o), axis=-1, keepdims=True) + EPS) * norm_w
                c0 = (2 * p + par) * GLA_DV
                mix_ref[rows, SWA_WIDTH + c0:SWA_WIDTH + c0 + GLA_DV] = (
                    o * gate_b[rows, c0:c0 + GLA_DV]).astype(BF16)

    def layer_norm_store(half, y):
        rows = slice(half * OUT_ROWS, (half + 1) * OUT_ROWS)
        y = y + ALPHA * x_ref[rows, :]
        mu = jnp.mean(y, axis=-1, keepdims=True)
        yc = y - mu
        var = jnp.mean(jnp.square(yc), axis=-1, keepdims=True)
        o_ref[rows, :] = yc * lax.rsqrt(var + EPS) * lng_ref[...] + lnb_ref[...]

    fill(len(pending))
    gate_a = jax.nn.silu(filled(GA))
    gate_b = jax.nn.silu(filled(GB))
    for n in range(per_half):
        value_dots(n)
    gated_outputs(0)
    out_piece = 2 * LANES
    n_piece = D_MODEL // out_piece
    y0 = []
    for n in range(per_half, 2 * per_half):
        value_dots(n)
        if n - per_half < n_piece:
            k = n - per_half
            y0.append(_mm(mix_ref[:OUT_ROWS, :], w_out_ref[:, k * out_piece:(k + 1) * out_piece]))
    layer_norm_store(0, jnp.concatenate(y0, axis=1))
    gated_outputs(1)
    layer_norm_store(1, _mm(mix_ref[OUT_ROWS:, :], w_out_ref[...]))


def _rope_freq_rows():
    half = ROT_DIM // 2
    inv_freq = ROPE_THETA ** (-jnp.arange(half, dtype=F32) / half)
    return jnp.broadcast_to(inv_freq[:, None], (half, BLOCK))


@jax.jit
def kernel(x, positions, w_in, gla_w_gate_up, gla_b_gate, attn_sinks, gla_norm_w, w_out, ln_g, ln_b):
    B, S, D = x.shape
    assert D == D_MODEL and S % TILE == 0 and w_in.shape == (DEPTH, D_MODEL, D_IN_PROJ)
    n_tiles = S // TILE
    pos_rows = positions.reshape(B * n_tiles, 1, TILE)

    def const(shape):
        return pl.BlockSpec((None,) + shape, lambda b, t: (0,) * (len(shape) + 1),
                            pipeline_mode=pl.Buffered(1))

    grid_spec = pl.GridSpec(
        grid=(B, n_tiles),
        in_specs=[
            pl.BlockSpec((None, TILE, D_MODEL), lambda b, t: (b, t, 0)),
            pl.BlockSpec((None, 1, TILE), lambda b, t: (b * n_tiles + t, 0, 0)),
            const((D_IN_PROJ, D_MODEL)),
            const((GLA_RANK, GLA_HEADS * GLA_DK)),
            const((1, GLA_HEADS * GLA_DK)),
            const((1, GLA_DV)),
            const((D_MODEL, D_MODEL)),
            const((1, D_MODEL)),
            const((1, D_MODEL)),
            pl.BlockSpec((ROT_DIM // 2, BLOCK), lambda b, t: (0, 0)),
            pl.BlockSpec(memory_space=pltpu.SMEM),
        ],
        out_specs=pl.BlockSpec((None, TILE, D_MODEL), lambda b, t: (b, t, 0)),
        scratch_shapes=[
            pltpu.VMEM((BLOCK, LANES), F32),
            pltpu.VMEM((BLOCK, LANES), F32),
            pltpu.VMEM((GLA_HEADS * GLA_DK, GLA_DV), F32),
            pltpu.VMEM((TILE, D_MODEL), BF16),
            pltpu.VMEM((RB // LANES, D_MODEL, LANES), BF16),
            pltpu.VMEM((D_MODEL, LANES), BF16),
            pltpu.VMEM((LANES, GLA_HEADS * GLA_DK), BF16),
            pltpu.VMEM((D_MODEL, D_MODEL), BF16),
            pltpu.VMEM((CUM_ROWS, CUM_ROWS), BF16),
        ],
    )
    row = lambda a: a.reshape(DEPTH, 1, -1)
    return pl.pallas_call(
        _layer_kernel,
        grid_spec=grid_spec,
        out_shape=jax.ShapeDtypeStruct((B, S, D_MODEL), x.dtype),
        compiler_params=pltpu.CompilerParams(
            dimension_semantics=("arbitrary", "arbitrary"),
            vmem_limit_bytes=VMEM_LIMIT_BYTES),
        name="hymba_layer",
    )(x, pos_rows, jnp.swapaxes(w_in, 1, 2), gla_w_gate_up, row(gla_b_gate), row(gla_norm_w),
      w_out, row(ln_g), row(ln_b), _rope_freq_rows(), attn_sinks[0])
```
